```python
import math
import jax
import jax.numpy as jnp
from jax import lax
import numpy as np

D_MODEL = 1024
BATCH = 8
SEQ = 4096
DEPTH = 4

GRID_W = 64
CTX_LEN = 256
F32 = jnp.float32
EPS = 1e-6

BRANCH_W = 512
N_BRANCH = 3

S5_WIDTH = BRANCH_W
S5_GROUP = 16
S5_GROUPS = S5_WIDTH // S5_GROUP
S5_STATE = 64
S5_DT_MIN = 0.001
S5_DT_MAX = 0.1
S5_C_STD = 0.5

HEAD_DIM = 64
ATT_HEADS = BRANCH_W // HEAD_DIM
ATT_KV_HEADS = 2
ATT_WINDOW = 128
ATT_BLOCK = 128
ROPE_BASE = 10000.0

HG_HEADS = 4
HG_DK = 128
HG_DV = BRANCH_W // HG_HEADS
HG_K = HG_HEADS * HG_DK
HG_V = HG_HEADS * HG_DV
HG_CHUNK = 64

FFN_DIM = 2816
CONV_W = 3

Q_W = ATT_HEADS * HEAD_DIM
KV_W = ATT_KV_HEADS * HEAD_DIM
IN_SPLITS = (S5_WIDTH, Q_W, KV_W, KV_W, HG_K, HG_K, HG_K, HG_V, HG_V, N_BRANCH * D_MODEL)
IN_COLS = sum(IN_SPLITS)

kernel_name = 'hybrid_s5_swa_hgrn2_dit_trunk'


def rmsnorm(x, g):
    xf = x.astype(F32)
    y = xf * lax.rsqrt(jnp.mean(xf * xf, axis=-1, keepdims=True) + EPS)
    return (y * g.astype(F32)).astype(x.dtype)


def modulate(h, shift, scale):
    return h * (1.0 + scale) + shift


def split_columns(z):
    idx = []
    acc = 0
    for w in IN_SPLITS[:-1]:
        acc += w
        idx.append(acc)
    return jnp.split(z, idx, axis=-1)


def _flip(t, rev):
    return jnp.flip(t, axis=1) if rev else t


def axial_rope(L):
    rows = L // GRID_W
    row = jnp.repeat(jnp.arange(rows, dtype=F32), GRID_W)
    col = jnp.tile(jnp.arange(GRID_W, dtype=F32), rows)
    nf = HEAD_DIM // 4
    inv = ROPE_BASE ** (-jnp.arange(nf, dtype=F32) / nf)
    ang = jnp.concatenate([row[:, None] * inv, col[:, None] * inv], axis=-1)
    return jnp.cos(ang), jnp.sin(ang)


def apply_rope(x, cos, sin):
    xf = x.astype(F32)
    x1, x2 = jnp.split(xf, 2, axis=-1)
    c = cos[None, :, None, :]
    s = sin[None, :, None, :]
    return jnp.concatenate([x1 * c - x2 * s, x2 * c + x1 * s], axis=-1).astype(x.dtype)


def s5_discretise(lam_re, lam_im, log_dt, b_re, b_im):
    lr = lam_re.astype(F32)
    li = lam_im.astype(F32)
    dt = jnp.exp(log_dt.astype(F32))[:, None]
    mag = jnp.exp(dt * lr)
    ar = mag * jnp.cos(dt * li)
    ai = mag * jnp.sin(dt * li)
    den = lr * lr + li * li
    fr = ((ar - 1.0) * lr + ai * li) / den
    fi = (ai * lr - (ar - 1.0) * li) / den
    br = b_re.astype(F32)
    bi = b_im.astype(F32)
    bbr = fr[..., None] * br - fi[..., None] * bi
    bbi = fr[..., None] * bi + fi[..., None] * br
    return ar, ai, bbr, bbi


def _complex_affine_combine(e1, e2):
    a1r, a1i, b1r, b1i = e1
    a2r, a2i, b2r, b2i = e2
    ar = a2r * a1r - a2i * a1i
    ai = a2r * a1i + a2i * a1r
    br = a2r * b1r - a2i * b1i + b2r
    bi = a2r * b1i + a2i * b1r + b2i
    return ar, ai, br, bi


def s5_scan(u, ar, ai, bbr, bbi, h0r, h0i):
    bur = jnp.einsum('blgc,gnc->blgn', u, bbr)
    bui = jnp.einsum('blgc,gnc->blgn', u, bbi)
    if h0r is not None:
        bur = bur.at[:, 0].add(ar * h0r - ai * h0i)
        bui = bui.at[:, 0].add(ar * h0i + ai * h0r)
    L = u.shape[1]
    a_r = jnp.broadcast_to(ar[None, None], (1, L) + ar.shape)
    a_i = jnp.broadcast_to(ai[None, None], (1, L) + ai.shape)
    _, _, hr, hi = lax.associative_scan(_complex_affine_combine, (a_r, a_i, bur, bui), axis=1)
    return hr, hi


def s5_readout(hr, hi, c_re, c_im):
    return (jnp.einsum('blgn,gcn->blgc', hr, c_re.astype(F32))
            - jnp.einsum('blgn,gcn->blgc', hi, c_im.astype(F32)))


def s5_glu(y, w_glu, b_glu):
    B, L = y.shape[0], y.shape[1]
    y = jax.nn.gelu(y.reshape(B, L, S5_WIDTH))
    return y * jax.nn.sigmoid(y @ w_glu.astype(F32) + b_glu.astype(F32))


def s5_branch(u_c, u_l, p, with_ctx_out):
    B, Lc, L = u_c.shape[0], u_c.shape[1], u_l.shape[1]
    uc = u_c.astype(F32).reshape(B, Lc, S5_GROUPS, S5_GROUP)
    ul = u_l.astype(F32).reshape(B, L, S5_GROUPS, S5_GROUP)
    dsk = p['s5_d'].astype(F32).reshape(S5_GROUPS, S5_GROUP)
    y_l = ul * dsk
    y_c = uc * dsk
    for d in range(2):
        rev = d == 1
        ar, ai, bbr, bbi = s5_discretise(p['s5_lam_re'][d], p['s5_lam_im'][d], p['s5_log_dt'][d],
                                         p['s5_b_re'], p['s5_b_im'])
        hcr, hci = s5_scan(_flip(uc, rev), ar, ai, bbr, bbi, None, None)
        hlr, hli = s5_scan(_flip(ul, rev), ar, ai, bbr, bbi, hcr[:, -1], hci[:, -1])
        y_l = y_l + _flip(s5_readout(hlr, hli, p['s5_c_re'], p['s5_c_im']), rev)
        if with_ctx_out:
            y_c = y_c + _flip(s5_readout(hcr, hci, p['s5_c_re'], p['s5_c_im']), rev)
    out_l = s5_glu(y_l, p['s5_w_glu'], p['s5_b_glu'])
    out_c = s5_glu(y_c, p['s5_w_glu'], p['s5_b_glu']) if with_ctx_out else None
    return out_l, out_c


def latent_window_attention(q, k, v, k_ctx, v_ctx, sink):
    B, L = q.shape[0], q.shape[1]
    Lc = k_ctx.shape[1]
    nb = L // ATT_BLOCK
    grp = ATT_HEADS // ATT_KV_HEADS
    scale = HEAD_DIM ** -0.5
    qb = q.reshape(B, nb, ATT_BLOCK, ATT_KV_HEADS, grp, HEAD_DIM)
    pad = ((0, 0), (ATT_BLOCK, ATT_BLOCK), (0, 0), (0, 0))
    kp = jnp.pad(k, pad).reshape(B, nb + 2, ATT_BLOCK, ATT_KV_HEADS, HEAD_DIM)
    vp = jnp.pad(v, pad).reshape(B, nb + 2, ATT_BLOCK, ATT_KV_HEADS, HEAD_DIM)
    kb = jnp.concatenate([kp[:, :-2], kp[:, 1:-1], kp[:, 2:]], axis=2)
    vb = jnp.concatenate([vp[:, :-2], vp[:, 1:-1], vp[:, 2:]], axis=2)
    s_loc = jnp.einsum('bnqhgd,bnkhd->bnhgqk', qb, kb).astype(F32) * scale
    qi = jnp.arange(ATT_BLOCK)[:, None]
    kj = jnp.arange(3 * ATT_BLOCK)[None, :] - ATT_BLOCK
    kabs = jnp.arange(nb)[:, None, None] * ATT_BLOCK + kj[None]
    valid = (jnp.abs(kj - qi) <= ATT_WINDOW)[None] & (kabs >= 0) & (kabs < L)
    s_loc = jnp.where(valid[None, :, None, None], s_loc, -jnp.inf)
    s_ctx = jnp.einsum('bnqhgd,bchd->bnhgqc', qb, k_ctx).astype(F32) * scale
    s_snk = jnp.broadcast_to(sink.astype(F32).reshape(ATT_KV_HEADS, grp, 1, 1), s_loc.shape[:-1] + (1,))
    p = jax.nn.softmax(jnp.concatenate([s_snk, s_ctx, s_loc], axis=-1), axis=-1).astype(v.dtype)
    o = (jnp.einsum('bnhgqc,bchd->bnqhgd', p[..., 1:1 + Lc], v_ctx)
         + jnp.einsum('bnhgqk,bnkhd->bnqhgd', p[..., 1 + Lc:], vb))
    return o.reshape(B, L, ATT_HEADS * HEAD_DIM)


def context_attention(q, k, v, sink):
    B, Lc = q.shape[0], q.shape[1]
    grp = ATT_HEADS // ATT_KV_HEADS
    qg = q.reshape(B, Lc, ATT_KV_HEADS, grp, HEAD_DIM)
    s = jnp.einsum('bqhgd,bkhd->bhgqk', qg, k).astype(F32) * (HEAD_DIM ** -0.5)
    s_snk = jnp.broadcast_to(sink.astype(F32).reshape(ATT_KV_HEADS, grp, 1, 1), s.shape[:-1] + (1,))
    p = jax.nn.softmax(jnp.concatenate([s_snk, s], axis=-1), axis=-1).astype(v.dtype)
    o = jnp.einsum('bhgqk,bkhd->bqhgd', p[..., 1:], v)
    return o.reshape(B, Lc, ATT_HEADS * HEAD_DIM)


def hgrn2_lower_bounds(logits):
    pr = jax.nn.softmax(logits.astype(F32), axis=0)
    cs = jnp.cumsum(pr, axis=0)
    return cs - cs[:1]


def hgrn2_forget(z, lb):
    zf = z.astype(F32)
    lbf = lb.astype(F32)
    logf = jnp.logaddexp(jnp.log(lbf), jnp.log1p(-lbf) + jax.nn.log_sigmoid(zf))
    k = (1.0 - lbf) * jax.nn.sigmoid(-zf)
    return logf, k


def _heads(t, d):
    return t.reshape(t.shape[0], t.shape[1], HG_HEADS, d)


def hgrn2_chunk_scan(q, logf, k, v, s0, readout):
    B, L, H, dk = q.shape
    dv = v.shape[-1]
    nc = L // HG_CHUNK

    def chunks(t):
        return jnp.moveaxis(t.reshape(B, nc, HG_CHUNK, H, t.shape[-1]), 1, 0)

    tri = jnp.tril(jnp.ones((HG_CHUNK, HG_CHUNK), dtype=bool))[None, :, :, None, None]

    def step(S, xs):
        qc, gc, kc, vc = xs
        b = jnp.cumsum(gc, axis=1)
        b_end = b[:, -1]
        S_next = (S * jnp.exp(b_end)[..., None]
                  + jnp.einsum('bchk,bchv->bhkv', kc * jnp.exp(b_end[:, None] - b), vc))
        if not readout:
            return S_next, None
        o_inter = jnp.einsum('bchk,bhkv->bchv', qc * jnp.exp(b), S)
        decay = jnp.exp(jnp.where(tri, b[:, :, None] - b[:, None, :], -jnp.inf))
        att = jnp.einsum('bthk,bshk,btshk->bhts', qc, kc, decay)
        return S_next, o_inter + jnp.einsum('bhts,bshv->bthv', att, vc)

    if s0 is None:
        s0 = jnp.zeros((B, H, dk, dv), F32)
    s_fin, o = lax.scan(step, s0, (chunks(q), chunks(logf), chunks(k), chunks(v)))
    if readout:
        o = jnp.moveaxis(o, 0, 1).reshape(B, L, H, dv)
    return s_fin, o


def hgrn2_output(o, g, norm_g):
    B, L = o.shape[0], o.shape[1]
    o = o * lax.rsqrt(jnp.mean(o * o, axis=-1, keepdims=True) + EPS)
    o = o.reshape(B, L, HG_V) * norm_g.astype(F32)
    return o * jax.nn.sigmoid(g.astype(F32))


def hgrn2_branch(zc, zl, lb_fwd, lb_bwd, norm_g, with_ctx_out):
    q_c, ff_c, fb_c, i_c, g_c = zc
    q_l, ff_l, fb_l, i_l, g_l = zl
    qc = _heads(jax.nn.silu(q_c.astype(F32)), HG_DK)
    ql = _heads(jax.nn.silu(q_l.astype(F32)), HG_DK)
    vc = _heads(i_c.astype(F32), HG_DV)
    vl = _heads(i_l.astype(F32), HG_DV)
    o_l = []
    o_c = []
    for f_c, f_l, lb, rev in ((ff_c, ff_l, lb_fwd, False), (fb_c, fb_l, lb_bwd, True)):
        lfc, kc = hgrn2_forget(f_c, lb)
        lfl, kl = hgrn2_forget(f_l, lb)
        s_c, oc = hgrn2_chunk_scan(_flip(qc, rev), _flip(_heads(lfc, HG_DK), rev),
                                   _flip(_heads(kc, HG_DK), rev), _flip(vc, rev), None, with_ctx_out)
        _, ol = hgrn2_chunk_scan(_flip(ql, rev), _flip(_heads(lfl, HG_DK), rev),
                                 _flip(_heads(kl, HG_DK), rev), _flip(vl, rev), s_c, True)
        o_l.append(_flip(ol, rev))
        if with_ctx_out:
            o_c.append(_flip(oc, rev))
    y_l = hgrn2_output(o_l[0] + o_l[1], g_l, norm_g)
    y_c = hgrn2_output(o_c[0] + o_c[1], g_c, norm_g) if with_ctx_out else None
    return y_l, y_c


def merge_branches(ya, yb, yc, gate_logits, w_branch, w_out, dtype):
    y = jnp.stack([ya.astype(dtype), yb.astype(dtype), yc.astype(dtype)], axis=2)
    z = jnp.einsum('blnw,nwd->blnd', y, w_branch)
    g = jax.nn.sigmoid(gate_logits.reshape(z.shape).astype(F32))
    m = jnp.sum(g * z.astype(F32), axis=2).astype(dtype)
    return m @ w_out


def mixing_sublayer(hl, hc, p, lb_fwd, lb_bwd, cos, sin, with_ctx_out):
    B, L = hl.shape[0], hl.shape[1]
    Lc = hc.shape[1]
    s5_l, q_l, k_l, v_l, hq_l, hff_l, hfb_l, hi_l, hg_l, gate_l = split_columns(hl @ p['w_in'])
    s5_c, q_c, k_c, v_c, hq_c, hff_c, hfb_c, hi_c, hg_c, gate_c = split_columns(hc @ p['w_in'])
    ya_l, ya_c = s5_branch(s5_c, s5_l, p, with_ctx_out)
    kh_c = k_c.reshape(B, Lc, ATT_KV_HEADS, HEAD_DIM)
    vh_c = v_c.reshape(B, Lc, ATT_KV_HEADS, HEAD_DIM)
    qh_l = apply_rope(q_l.reshape(B, L, ATT_HEADS, HEAD_DIM), cos, sin)
    kh_l = apply_rope(k_l.reshape(B, L, ATT_KV_HEADS, HEAD_DIM), cos, sin)
    vh_l = v_l.reshape(B, L, ATT_KV_HEADS, HEAD_DIM)
    yb_l = latent_window_attention(qh_l, kh_l, vh_l, kh_c, vh_c, p['att_sink'])
    yc_l, yc_c = hgrn2_branch((hq_c, hff_c, hfb_c, hi_c, hg_c), (hq_l, hff_l, hfb_l, hi_l, hg_l),
                              lb_fwd, lb_bwd, p['hg_norm_g'], with_ctx_out)
    out_l = merge_branches(ya_l, yb_l, yc_l, gate_l, p['w_branch'], p['w_out'], hl.dtype)
    out_c = None
    if with_ctx_out:
        yb_c = context_attention(q_c.reshape(B, Lc, ATT_HEADS, HEAD_DIM), kh_c, vh_c, p['att_sink'])
        out_c = merge_branches(ya_c, yb_c, yc_c, gate_c, p['w_branch'], p['w_out'], hc.dtype)
    return out_l, out_c


def depthwise_conv(u, w, b):
    y = lax.conv_general_dilated(u, w[:, None, :].astype(u.dtype), window_strides=(1,), padding='SAME',
                                 dimension_numbers=('NWC', 'WIO', 'NWC'), feature_group_count=u.shape[-1])
    return y + b


def conv_ffn(h, w_up, conv_w, conv_b, w_down):
    u = depthwise_conv(h @ w_up, conv_w, conv_b)
    a, g = jnp.split(u, 2, axis=-1)
    return (jax.nn.silu(a) * g) @ w_down


def setup_inputs(seed: int = 0) -> dict:
    key = jax.random.key(seed)
    ks = jax.random.split(key, 28)
    D = D_MODEL

    def nrm(k, shape, scale):
        return jax.random.normal(k, shape, F32) * scale

    s5_shape = (DEPTH, 2, S5_GROUPS, S5_STATE)
    return {
        'x': nrm(ks[0], (BATCH, SEQ, D), 1.0),
        'c': nrm(ks[1], (BATCH, D), 1.0),
        'ctx': nrm(ks[2], (BATCH, CTX_LEN, D), 1.0),
        'c_ctx': nrm(ks[3], (D,), 1.0),
        'w_mod': nrm(ks[4], (DEPTH, D, 6 * D), 0.5 * D ** -0.5),
        'b_mod': nrm(ks[5], (DEPTH, 6 * D), 0.02),
        'norm_g': 1.0 + nrm(ks[6], (DEPTH, 4, D), 0.02),
        'w_in': nrm(ks[7], (DEPTH, D, IN_COLS), D ** -0.5),
        's5_lam_re': -0.5 + nrm(ks[8], s5_shape, 0.01),
        's5_lam_im': jnp.pi * jnp.arange(S5_STATE, dtype=F32) + nrm(ks[9], s5_shape, 0.01),
        's5_log_dt': jax.random.uniform(ks[10], (DEPTH, 2, S5_GROUPS), F32,
                                        math.log(S5_DT_MIN), math.log(S5_DT_MAX)),
        's5_b_re': nrm(ks[11], (DEPTH, S5_GROUPS, S5_STATE, S5_GROUP), (2 * S5_GROUP) ** -0.5),
        's5_b_im': nrm(ks[12], (DEPTH, S5_GROUPS, S5_STATE, S5_GROUP), (2 * S5_GROUP) ** -0.5),
        's5_c_re': nrm(ks[13], (DEPTH, S5_GROUPS, S5_GROUP, S5_STATE), S5_C_STD),
        's5_c_im': nrm(ks[14], (DEPTH, S5_GROUPS, S5_GROUP, S5_STATE), S5_C_STD),
        's5_d': nrm(ks[15], (DEPTH, S5_WIDTH), 1.0),
        's5_w_glu': nrm(ks[16], (DEPTH, S5_WIDTH, S5_WIDTH), S5_WIDTH ** -0.5),
        's5_b_glu': nrm(ks[17], (DEPTH, S5_WIDTH), 0.02),
        'att_sink': nrm(ks[18], (DEPTH, ATT_HEADS), 0.5),
        'hg_lb_logits': nrm(ks[19], (DEPTH, 2, HG_K), 0.5),
        'hg_norm_g': 1.0 + nrm(ks[20], (DEPTH, HG_V), 0.02),
        'w_branch': nrm(ks[21], (DEPTH, N_BRANCH, BRANCH_W, D), BRANCH_W ** -0.5),
        'w_out': nrm(ks[22], (DEPTH, D, D), D ** -0.5),
        'ffn_w_up': nrm(ks[23], (DEPTH, D, 2 * FFN_DIM), D ** -0.5),
        'ffn_conv_w': nrm(ks[24], (DEPTH, CONV_W, 2 * FFN_DIM), CONV_W ** -0.5),
        'ffn_conv_b': nrm(ks[25], (DEPTH, 2 * FFN_DIM), 0.02),
        'ffn_w_down': nrm(ks[26], (DEPTH, FFN_DIM, D), FFN_DIM ** -0.5),
    }


def reference(x, c, ctx, c_ctx, w_mod, b_mod, norm_g, w_in, s5_lam_re, s5_lam_im, s5_log_dt,
              s5_b_re, s5_b_im, s5_c_re, s5_c_im, s5_d, s5_w_glu, s5_b_glu, att_sink,
              hg_lb_logits, hg_norm_g, w_branch, w_out, ffn_w_up, ffn_conv_w, ffn_conv_b, ffn_w_down):
    L = x.shape[1]
    cos, sin = axial_rope(L)
    lb_all = hgrn2_lower_bounds(hg_lb_logits)
    xl = x
    xc = ctx
    for l in range(DEPTH):
        with_ctx_out = l < DEPTH - 1
        p = {
            'w_in': w_in[l], 's5_lam_re': s5_lam_re[l], 's5_lam_im': s5_lam_im[l],
            's5_log_dt': s5_log_dt[l], 's5_b_re': s5_b_re[l], 's5_b_im': s5_b_im[l],
            's5_c_re': s5_c_re[l], 's5_c_im': s5_c_im[l], 's5_d': s5_d[l],
            's5_w_glu': s5_w_glu[l], 's5_b_glu': s5_b_glu[l], 'att_sink': att_sink[l],
            'hg_norm_g': hg_norm_g[l], 'w_branch': w_branch[l], 'w_out': w_out[l],
        }
        mod_l = (jax.nn.silu(c) @ w_mod[l] + b_mod[l])[:, None, :]
        mod_c = jax.nn.silu(c_ctx) @ w_mod[l] + b_mod[l]
        sh1_l, sc1_l, g1_l, sh2_l, sc2_l, g2_l = jnp.split(mod_l, 6, axis=-1)
        sh1_c, sc1_c, g1_c, sh2_c, sc2_c, g2_c = jnp.split(mod_c, 6, axis=-1)
        hl = modulate(rmsnorm(xl, norm_g[l, 0]), sh1_l, sc1_l)
        hc = modulate(rmsnorm(xc, norm_g[l, 0]), sh1_c, sc1_c)
        ol, oc = mixing_sublayer(hl, hc, p, lb_all[l, 0], lb_all[l, 1], cos, sin, with_ctx_out)
        xl = xl + g1_l * rmsnorm(ol, norm_g[l, 1])
        hl = modulate(rmsnorm(xl, norm_g[l, 2]), sh2_l, sc2_l)
        xl = xl + g2_l * rmsnorm(conv_ffn(hl, ffn_w_up[l], ffn_conv_w[l], ffn_conv_b[l], ffn_w_down[l]),
                                 norm_g[l, 3])
        if with_ctx_out:
            xc = xc + g1_c * rmsnorm(oc, norm_g[l, 1])
            hc = modulate(rmsnorm(xc, norm_g[l, 2]), sh2_c, sc2_c)
            xc = xc + g2_c * rmsnorm(conv_ffn(hc, ffn_w_up[l], ffn_conv_w[l], ffn_conv_b[l], ffn_w_down[l]),
                                     norm_g[l, 3])
    return xl
```

```python
import functools
import math

import jax
import jax.numpy as jnp
from jax import lax
from jax.experimental import pallas as pl
from jax.experimental.pallas import tpu as pltpu

F32 = jnp.float32
MXU_DT = jnp.bfloat16
ACT_DT = jnp.bfloat16
EPS = 1e-6

BRANCH_W = 512
S5_GROUP = 16
S5_GROUPS = BRANCH_W // S5_GROUP
S5_STATE = 64
HEAD_DIM = 64
ATT_HEADS = 8
ATT_KV_HEADS = 2
ATT_GRP = ATT_HEADS // ATT_KV_HEADS
ATT_BLOCK = 128
GRID_W = 64
ROPE_BASE = 10000.0
HG_HEADS = 4
HG_D = 128
CONV_W = 3

LANES = 128
SUBLANES = 8
VMEM_LIMIT = 56 * 1024 * 1024

S5_T = 8
S5_GPT = LANES // S5_GROUP
S5_NQ = BRANCH_W // LANES
S5_SW = S5_GPT * S5_STATE
HG_CHUNK = 64
HG_MID = HG_CHUNK // 2


def _cparams(*sem):
    return pltpu.CompilerParams(dimension_semantics=sem, vmem_limit_bytes=VMEM_LIMIT)


def _rms(x):
    return x * lax.rsqrt(jnp.mean(x * x, axis=-1, keepdims=True) + EPS)


def _mod_kernel(a_ref, w_ref, b_ref, o_ref):
    a = a_ref[...]
    a = a * jax.nn.sigmoid(a)
    o_ref[0] = lax.dot_general(a, w_ref[0], (((1,), (0,)), ((), ())),
                               precision=lax.Precision.HIGHEST,
                               preferred_element_type=F32) + b_ref[0]


def _mod_all(cond, w_mod, b_mod):
    depth, d, n = w_mod.shape
    r = cond.shape[0]
    tn = 1536
    return pl.pallas_call(
        _mod_kernel,
        out_shape=jax.ShapeDtypeStruct((depth, r, n), F32),
        grid=(depth, n // tn),
        in_specs=[pl.BlockSpec((r, d), lambda l, j: (0, 0)),
                  pl.BlockSpec((1, d, tn), lambda l, j: (l, 0, j)),
                  pl.BlockSpec((1, 1, tn), lambda l, j: (l, 0, j))],
        out_specs=pl.BlockSpec((1, r, tn), lambda l, j: (l, 0, j)),
        compiler_params=_cparams("parallel", "parallel"),
        name="adaln_mod",
    )(cond, w_mod, b_mod.reshape(depth, 1, n))


_C_U = (0, 512)
_C_Q = (512, 1024)
_C_K = (1024, 1152)
_C_V = (1152, 1280)
_C_HQ = (1280, 1792)
_C_FF = (1792, 2304)
_C_FB = (2304, 2816)
_C_HI = (2816, 3328)
_C_HG = (3328, 3840)
_C_GATE = (3840, 6912)


def _rope(z, cos, sin):
    lane = lax.broadcasted_iota(jnp.int32, z.shape, 1)
    first = (lane & (HEAD_DIM // 2)) == 0
    partner = jnp.where(first, pltpu.roll(z, LANES - HEAD_DIM // 2, 1), pltpu.roll(z, HEAD_DIM // 2, 1))
    return z * cos + partner * sin


def _win_kernel(x_ref, sh_ref, sc_ref, g_ref, w_ref, cos_ref, sin_ref,
                u_ref, q_ref, k_ref, v_ref, hq_ref, ff_ref, fb_ref, hi_ref, hg_ref, gate_ref):
    x = x_ref[0]
    h = _rms(x) * g_ref[...]
    h = h * (1.0 + sc_ref[0]) + sh_ref[0]
    hb = h.astype(MXU_DT)

    def mm(lo, hi):
        return jnp.dot(hb, w_ref[:, lo:hi], preferred_element_type=F32)

    u_ref[0] = mm(*_C_U)
    cos = cos_ref[...]
    sin = sin_ref[...]
    for s in range(BRANCH_W // LANES):
        lo = _C_Q[0] + s * LANES
        z = _rope(mm(lo, lo + LANES), cos, sin) * (HEAD_DIM ** -0.5)
        q_ref[0, :, s * LANES:(s + 1) * LANES] = z.astype(q_ref.dtype)
    k_ref[0] = _rope(mm(*_C_K), cos, sin).astype(k_ref.dtype)
    v_ref[0] = mm(*_C_V).astype(v_ref.dtype)
    z = mm(*_C_HQ)
    hq_ref[0] = (z * jax.nn.sigmoid(z)).astype(hq_ref.dtype)
    ff_ref[0] = mm(*_C_FF)
    fb_ref[0] = mm(*_C_FB)
    hi_ref[0] = mm(*_C_HI).astype(hi_ref.dtype)
    hg_ref[0] = jax.nn.sigmoid(mm(*_C_HG)).astype(hg_ref.dtype)
    for s in range((_C_GATE[1] - _C_GATE[0]) // BRANCH_W):
        lo = _C_GATE[0] + s * BRANCH_W
        gate_ref[0, :, s * BRANCH_W:(s + 1) * BRANCH_W] = (
            jax.nn.sigmoid(mm(lo, lo + BRANCH_W)).astype(gate_ref.dtype))


def _win(x, shift, scale, gain, w, cos, sin):
    b, l, d = x.shape
    tm = min(256, l)
    n = w.shape[1]
    widths = [(512, F32), (512, ACT_DT), (128, ACT_DT), (128, ACT_DT), (512, ACT_DT),
              (512, F32), (512, F32), (512, ACT_DT), (512, ACT_DT), (3072, ACT_DT)]
    row = lambda bi, i: (bi, i, 0)
    vec = lambda bi, i: (bi, 0, 0)
    return pl.pallas_call(
        _win_kernel,
        out_shape=[jax.ShapeDtypeStruct((b, l, wd), dt) for wd, dt in widths],
        grid=(b, l // tm),
        in_specs=[pl.BlockSpec((1, tm, d), row),
                  pl.BlockSpec((1, 1, d), vec),
                  pl.BlockSpec((1, 1, d), vec),
                  pl.BlockSpec((1, d), lambda bi, i: (0, 0)),
                  pl.BlockSpec((d, n), lambda bi, i: (0, 0)),
                  pl.BlockSpec((tm, LANES), lambda bi, i: (i, 0)),
                  pl.BlockSpec((tm, LANES), lambda bi, i: (i, 0))],
        out_specs=[pl.BlockSpec((1, tm, wd), row) for wd, _ in widths],
        compiler_params=_cparams("parallel", "parallel"),
        name="in_proj",
    )(x, shift, scale, gain, w, cos, sin)


def _s5_tables(lam_re, lam_im, log_dt, b_re, b_im, c_re, c_im, d_skip):
    t_, g_, n_, c_ = S5_T, S5_GROUPS, S5_STATE, S5_GROUP
    hp = lax.Precision.HIGHEST
    lr = lam_re.astype(F32)
    li = lam_im.astype(F32)
    dt = jnp.exp(log_dt.astype(F32))[..., None]
    xr, xi = dt * lr, dt * li
    mag = jnp.exp(xr)
    ar, ai = mag * jnp.cos(xi), mag * jnp.sin(xi)
    den = lr * lr + li * li
    fr = ((ar - 1.0) * lr + ai * li) / den
    fi = (ai * lr - (ar - 1.0) * li) / den
    br, bi = b_re.astype(F32), b_im.astype(F32)
    bbr = fr[..., None] * br - fi[..., None] * bi
    bbi = fr[..., None] * bi + fi[..., None] * br
    kk = jnp.arange(t_ + 1, dtype=F32)[:, None, None, None]
    pr = jnp.exp(kk * xr) * jnp.cos(kk * xi)
    pi = jnp.exp(kk * xr) * jnp.sin(kk * xi)
    wr = pr[..., None] * bbr - pi[..., None] * bbi
    wi = pr[..., None] * bbi + pi[..., None] * bbr
    cr, ci = c_re.astype(F32), c_im.astype(F32)
    kern = (jnp.einsum('gon,kdgni->kdgoi', cr, wr[:t_], precision=hp)
            - jnp.einsum('gon,kdgni->kdgoi', ci, wi[:t_], precision=hp))
    s_idx = jnp.arange(t_)[:, None]
    t_idx = jnp.arange(t_)[None, :]
    lag_f = jnp.clip(t_idx - s_idx, 0, t_ - 1)
    lag_b = jnp.clip(s_idx - t_idx, 0, t_ - 1)
    kf = jnp.where((t_idx >= s_idx)[..., None, None, None], kern[lag_f, 0], 0.0)
    kb = jnp.where((s_idx >= t_idx)[..., None, None, None], kern[lag_b, 1], 0.0)
    skip = (jnp.eye(t_, dtype=F32)[:, :, None, None, None]
            * jnp.eye(c_, dtype=F32)[None, None, None]
            * d_skip.astype(F32).reshape(g_, c_)[None, None, :, None, :])
    ktoe = (kf + kb + skip).reshape(t_, t_, S5_NQ, S5_GPT, c_, c_)
    eye = jnp.eye(S5_GPT, dtype=F32)
    toe = jnp.einsum('stqgoi,gh->qsgitho', ktoe, eye).reshape(S5_NQ, t_ * LANES, t_ * LANES)

    qsel_r = jnp.stack([wr[:t_, 0][::-1], wr[:t_, 1]], 0)
    qsel_i = jnp.stack([wi[:t_, 0][::-1], wi[:t_, 1]], 0)
    qsel = jnp.stack([qsel_r, qsel_i], 1).reshape(2, 2, t_, S5_NQ, S5_GPT, n_, c_)
    qm = jnp.einsum('drsqgni,gh->qsgidrhn', qsel, eye).reshape(S5_NQ, t_ * LANES, 4 * S5_SW)

    psr = jnp.stack([pr[1:, 0], pr[1:, 1][::-1]], 0)
    psi = jnp.stack([pi[1:, 0], pi[1:, 1][::-1]], 0)
    crt = jnp.swapaxes(cr, 1, 2)
    cit = jnp.swapaxes(ci, 1, 2)
    p_re = crt[None, None] * psr[..., None] - cit[None, None] * psi[..., None]
    p_im = -crt[None, None] * psi[..., None] - cit[None, None] * psr[..., None]
    psel = jnp.stack([p_re, p_im], 1).reshape(2, 2, t_, S5_NQ, S5_GPT, n_, c_)
    pm = jnp.einsum('drtqgno,gh->qdrgntho', psel, eye).reshape(S5_NQ, 4 * S5_SW, t_ * LANES)

    dsel = jnp.stack([pr[t_], pi[t_]], 1).reshape(2, 2, S5_NQ, S5_GPT * n_)
    dec = jnp.transpose(dsel, (2, 0, 1, 3)).reshape(S5_NQ, 1, 4 * S5_SW)
    return toe.astype(MXU_DT), qm.astype(MXU_DT), pm.astype(MXU_DT), dec


def _s5_kernel(u_ref, toe_ref, qm_ref, pm_ref, dec_ref, h0_ref, y_ref, hend_ref,
               z_ref, s_ref, hin_ref, *, nj):
    t_, sw = S5_T, S5_SW
    for t in range(t_):
        z_ref[:, t * LANES:(t + 1) * LANES] = u_ref[0, pl.ds(t, nj, stride=t_), :].astype(z_ref.dtype)
    s_ref[...] = jnp.dot(z_ref[...], qm_ref[0], preferred_element_type=F32)
    dec = dec_ref[0]
    dfr, dfi, dbr, dbi = (dec[:, i * sw:(i + 1) * sw] for i in range(4))
    h0 = h0_ref[0, 0]

    def step(j, carry):
        fr, fi, br, bi = carry
        jb = nj - 1 - j
        hin_ref[pl.ds(j, 1), 0 * sw:1 * sw] = fr
        hin_ref[pl.ds(j, 1), 1 * sw:2 * sw] = fi
        hin_ref[pl.ds(jb, 1), 2 * sw:3 * sw] = br
        hin_ref[pl.ds(jb, 1), 3 * sw:4 * sw] = bi
        sfr = s_ref[pl.ds(j, 1), 0 * sw:1 * sw]
        sfi = s_ref[pl.ds(j, 1), 1 * sw:2 * sw]
        sbr = s_ref[pl.ds(jb, 1), 2 * sw:3 * sw]
        sbi = s_ref[pl.ds(jb, 1), 3 * sw:4 * sw]
        return (dfr * fr - dfi * fi + sfr, dfr * fi + dfi * fr + sfi,
                dbr * br - dbi * bi + sbr, dbr * bi + dbi * br + sbi)

    fin = lax.fori_loop(0, nj, step, tuple(h0[:, i * sw:(i + 1) * sw] for i in range(4)))
    for i in range(4):
        hend_ref[0, 0, :, i * sw:(i + 1) * sw] = fin[i]
    y = (jnp.dot(z_ref[...], toe_ref[0], preferred_element_type=F32)
         + jnp.dot(hin_ref[...].astype(MXU_DT), pm_ref[0], preferred_element_type=F32))
    for t in range(t_):
        y_ref[0, pl.ds(t, nj, stride=t_), :] = y[:, t * LANES:(t + 1) * LANES]


def _s5(u, tables, h0):
    toe, qm, pm, dec = tables
    b, l, _ = u.shape
    nj = l // S5_T
    kw = S5_T * LANES
    sw4 = 4 * S5_SW
    wmap = lambda q, bi: (q, 0, 0)
    return pl.pallas_call(
        functools.partial(_s5_kernel, nj=nj),
        out_shape=[jax.ShapeDtypeStruct((b, l, BRANCH_W), F32),
                   jax.ShapeDtypeStruct((b, S5_NQ, 1, sw4), F32)],
        grid=(S5_NQ, b),
        in_specs=[pl.BlockSpec((1, l, LANES), lambda q, bi: (bi, 0, q)),
                  pl.BlockSpec((1, kw, kw), wmap),
                  pl.BlockSpec((1, kw, sw4), wmap),
                  pl.BlockSpec((1, sw4, kw), wmap),
                  pl.BlockSpec((1, 1, sw4), wmap),
                  pl.BlockSpec((1, 1, 1, sw4), lambda q, bi: (bi, q, 0, 0))],
        out_specs=[pl.BlockSpec((1, l, LANES), lambda q, bi: (bi, 0, q)),
                   pl.BlockSpec((1, 1, 1, sw4), lambda q, bi: (bi, q, 0, 0))],
        scratch_shapes=[pltpu.VMEM((nj, kw), MXU_DT),
                        pltpu.VMEM((nj, sw4), F32),
                        pltpu.VMEM((nj, sw4), F32)],
        compiler_params=_cparams("parallel", "parallel"),
        name="s5_mix",
    )(u, toe, qm, pm, dec, h0)


def _softmax_pv(s, sink_col, v):
    m = jnp.maximum(jnp.max(s, axis=-1, keepdims=True), sink_col)
    p = jnp.exp(s - m)
    den = jnp.sum(p, axis=-1, keepdims=True) + jnp.exp(sink_col - m)
    o = jnp.dot(p.astype(MXU_DT), v, preferred_element_type=F32)
    return o / den


def _attn_kernel(sink_ref, q_ref, kp_ref, kc_ref, kn_ref, vp_ref, vc_ref, vn_ref, kx_ref, vx_ref,
                 o_ref, *, nb, lc):
    i = pl.program_id(1)
    blk = ATT_BLOCK
    neg = jnp.float32(-jnp.inf)
    qi = lax.broadcasted_iota(jnp.int32, (blk, blk), 0)
    kj = lax.broadcasted_iota(jnp.int32, (blk, blk), 1)
    bias_p = jnp.where(i > 0, jnp.where(kj >= qi, 0.0, neg), neg)
    bias_n = jnp.where(i < nb - 1, jnp.where(kj <= qi, 0.0, neg), neg)
    bias = jnp.concatenate([jnp.zeros((blk, lc), F32), bias_p, jnp.zeros((blk, blk), F32), bias_n], axis=1)
    bias = jnp.concatenate([bias] * ATT_GRP, axis=0)
    outs = []
    for hk in range(ATT_KV_HEADS):
        ks = slice(hk * HEAD_DIM, (hk + 1) * HEAD_DIM)
        kcat = jnp.concatenate([kx_ref[0, :, ks], kp_ref[0, :, ks], kc_ref[0, :, ks], kn_ref[0, :, ks]], axis=0)
        vcat = jnp.concatenate([vx_ref[0, :, ks], vp_ref[0, :, ks], vc_ref[0, :, ks], vn_ref[0, :, ks]], axis=0)
        qs = jnp.concatenate(
            [q_ref[0, :, (hk * ATT_GRP + g) * HEAD_DIM:(hk * ATT_GRP + g + 1) * HEAD_DIM] for g in range(ATT_GRP)],
            axis=0)
        sink_col = jnp.concatenate(
            [jnp.full((blk, 1), sink_ref[hk * ATT_GRP + g], F32) for g in range(ATT_GRP)], axis=0)
        s = lax.dot_general(qs, kcat, (((1,), (1,)), ((), ())), preferred_element_type=F32) + bias
        o = _softmax_pv(s, sink_col, vcat)
        outs.extend(o[g * blk:(g + 1) * blk] for g in range(ATT_GRP))
    o_ref[0] = jnp.concatenate(outs, axis=1).astype(o_ref.dtype)


def _attn(q, k, v, kx, vx, sink):
    b, l, _ = q.shape
    lc = kx.shape[1]
    nb = l // ATT_BLOCK
    kvw = ATT_KV_HEADS * HEAD_DIM
    cur = lambda bi, i: (bi, i, 0)
    prev = lambda bi, i: (bi, jnp.maximum(i - 1, 0), 0)
    nxt = lambda bi, i: (bi, jnp.minimum(i + 1, nb - 1), 0)
    ctx = lambda bi, i: (bi, 0, 0)
    kvspec = lambda m: pl.BlockSpec((1, ATT_BLOCK, kvw), m)
    return pl.pallas_call(
        functools.partial(_attn_kernel, nb=nb, lc=lc),
        out_shape=jax.ShapeDtypeStruct((b, l, BRANCH_W), ACT_DT),
        grid=(b, nb),
        in_specs=[pl.BlockSpec(memory_space=pltpu.SMEM),
                  pl.BlockSpec((1, ATT_BLOCK, BRANCH_W), cur),
                  kvspec(prev), kvspec(cur), kvspec(nxt),
                  kvspec(prev), kvspec(cur), kvspec(nxt),
                  pl.BlockSpec((1, lc, kvw), ctx),
                  pl.BlockSpec((1, lc, kvw), ctx)],
        out_specs=pl.BlockSpec((1, ATT_BLOCK, BRANCH_W), cur),
        compiler_params=_cparams("parallel", "parallel"),
        name="window_attn",
    )(sink, q, k, k, k, v, v, v, kx, vx)


def _attn_ctx_kernel(sink_ref, q_ref, k_ref, v_ref, o_ref, *, lc):
    outs = []
    for hk in range(ATT_KV_HEADS):
        ks = slice(hk * HEAD_DIM, (hk + 1) * HEAD_DIM)
        kk = k_ref[0, :, ks]
        vv = v_ref[0, :, ks]
        qs = jnp.concatenate(
            [q_ref[0, :, (hk * ATT_GRP + g) * HEAD_DIM:(hk * ATT_GRP + g + 1) * HEAD_DIM] for g in range(ATT_GRP)],
            axis=0)
        sink_col = jnp.concatenate(
            [jnp.full((lc, 1), sink_ref[hk * ATT_GRP + g], F32) for g in range(ATT_GRP)], axis=0)
        s = lax.dot_general(qs, kk, (((1,), (1,)), ((), ())), preferred_element_type=F32)
        o = _softmax_pv(s, sink_col, vv)
        outs.extend(o[g * lc:(g + 1) * lc] for g in range(ATT_GRP))
    o_ref[0] = jnp.concatenate(outs, axis=1).astype(o_ref.dtype)


def _attn_ctx(q, k, v, sink):
    b, lc, _ = q.shape
    kvw = ATT_KV_HEADS * HEAD_DIM
    full = lambda bi: (bi, 0, 0)
    return pl.pallas_call(
        functools.partial(_attn_ctx_kernel, lc=lc),
        out_shape=jax.ShapeDtypeStruct((b, lc, BRANCH_W), ACT_DT),
        grid=(b,),
        in_specs=[pl.BlockSpec(memory_space=pltpu.SMEM),
                  pl.BlockSpec((1, lc, BRANCH_W), full),
                  pl.BlockSpec((1, lc, kvw), full),
                  pl.BlockSpec((1, lc, kvw), full)],
        out_specs=pl.BlockSpec((1, lc, BRANCH_W), full),
        compiler_params=_cparams("parallel"),
        name="ctx_attn",
    )(sink, q, k, v)


def _split3(x):
    a = x.astype(jnp.bfloat16)
    r = x - a.astype(F32)
    b = r.astype(jnp.bfloat16)
    c = (r - b.astype(F32)).astype(jnp.bfloat16)
    return a, b, c


def _hg_chunk(q, z, v, lb, tri, st_ref, d, reverse):
    c = HG_CHUNK
    s = jax.nn.sigmoid(z)
    f = lb + (1.0 - lb) * s
    logf = jnp.log(f)
    kf = (1.0 - lb) * (1.0 - s)
    cum = sum(jnp.dot(tri, part, preferred_element_type=F32) for part in _split3(logf))
    end_row = 0 if reverse else c - 1
    mid_row = HG_MID if reverse else HG_MID - 1
    cum_end = cum[end_row:end_row + 1]
    cum_mid = cum[mid_row:mid_row + 1]
    rel = cum - cum_mid
    qf = q.astype(F32)
    q_in = (qf * jnp.exp(rel)).astype(MXU_DT)
    k_in = (kf * jnp.exp(-rel)).astype(MXU_DT)
    q_st = (qf * jnp.exp(cum)).astype(MXU_DT)
    k_st = (kf * jnp.exp(cum_end - cum)).astype(MXU_DT)
    vb = v.astype(MXU_DT)
    dec = jnp.exp(cum_end)
    outs = []
    for h in range(HG_HEADS):
        hs = slice(h * HG_D, (h + 1) * HG_D)
        att = lax.dot_general(q_in[:, hs], k_in[:, hs], (((1,), (1,)), ((), ())), preferred_element_type=F32)
        att = jnp.where(tri > 0, att, 0.0).astype(MXU_DT)
        st = st_ref[d, h]
        o = jnp.dot(att, vb[:, hs], preferred_element_type=F32)
        o = o + lax.dot_general(q_st[:, hs], st.astype(MXU_DT), (((1,), (1,)), ((), ())),
                                preferred_element_type=F32)
        st_ref[d, h] = st * dec[:, hs] + lax.dot_general(vb[:, hs], k_st[:, hs], (((0,), (0,)), ((), ())),
                                                         preferred_element_type=F32)
        outs.append(o)
    return jnp.concatenate(outs, axis=1)


def _hgrn_kernel(qf_ref, qb_ref, vf_ref, vb_ref, ff_ref, fb_ref, lb_ref, s0_ref,
                 of_ref, ob_ref, send_ref, st_ref, *, nblk, cpb):
    j = pl.program_id(1)
    c = HG_CHUNK

    @pl.when(j == 0)
    def _():
        st_ref[...] = s0_ref[0]

    r = lax.broadcasted_iota(jnp.int32, (c, c), 0)
    col = lax.broadcasted_iota(jnp.int32, (c, c), 1)
    tri_f = (r >= col).astype(MXU_DT)
    tri_b = (r <= col).astype(MXU_DT)
    lbf = lb_ref[0:1]
    lbb = lb_ref[1:2]
    for ci in range(cpb):
        fs = slice(ci * c, (ci + 1) * c)
        bs = slice((cpb - 1 - ci) * c, (cpb - ci) * c)
        of_ref[0, fs] = _hg_chunk(qf_ref[0, fs], ff_ref[0, fs], vf_ref[0, fs], lbf, tri_f, st_ref, 0,
                                  False).astype(of_ref.dtype)
        ob_ref[0, bs] = _hg_chunk(qb_ref[0, bs], fb_ref[0, bs], vb_ref[0, bs], lbb, tri_b, st_ref, 1,
                                  True).astype(ob_ref.dtype)

    @pl.when(j == nblk - 1)
    def _():
        send_ref[0] = st_ref[...]


def _hgrn(hq, hi, ff, fb, lb, s0):
    b, l, w = hq.shape
    tb = min(256, l)
    nblk = l // tb
    fw = lambda bi, j: (bi, j, 0)
    bw = lambda bi, j: (bi, nblk - 1 - j, 0)
    st = lambda bi, j: (bi, 0, 0, 0, 0)
    blk = lambda m: pl.BlockSpec((1, tb, w), m)
    return pl.pallas_call(
        functools.partial(_hgrn_kernel, nblk=nblk, cpb=tb // HG_CHUNK),
        out_shape=[jax.ShapeDtypeStruct((b, l, w), ACT_DT),
                   jax.ShapeDtypeStruct((b, l, w), ACT_DT),
                   jax.ShapeDtypeStruct(s0.shape, F32)],
        grid=(b, nblk),
        in_specs=[blk(fw), blk(bw), blk(fw), blk(bw), blk(fw), blk(bw),
                  pl.BlockSpec((2, w), lambda bi, j: (0, 0)),
                  pl.BlockSpec((1,) + s0.shape[1:], st)],
        out_specs=[blk(fw), blk(bw), pl.BlockSpec((1,) + s0.shape[1:], st)],
        scratch_shapes=[pltpu.VMEM(s0.shape[1:], F32)],
        compiler_params=_cparams("parallel", "arbitrary"),
        name="hgrn2_scan",
    )(hq, hq, hi, hi, ff, fb, lb, s0)


def _merge_kernel(x_ref, g1_ref, y5_ref, att_ref, of_ref, ob_ref, hg_ref, gate_ref,
                  wglu_ref, bglu_ref, hgn_ref, wbr_ref, wout_ref, gpost_ref, o_ref):
    d = x_ref.shape[-1]
    ge = jax.nn.gelu(y5_ref[0])
    ya = ge * jax.nn.sigmoid(jnp.dot(ge.astype(MXU_DT), wglu_ref[...], preferred_element_type=F32)
                             + bglu_ref[...])
    o = of_ref[0].astype(F32) + ob_ref[0].astype(F32)
    o = jnp.concatenate([_rms(o[:, h * HG_D:(h + 1) * HG_D]) for h in range(HG_HEADS)], axis=1)
    yc = o * hgn_ref[...] * hg_ref[0].astype(F32)
    ys = (ya.astype(MXU_DT), att_ref[0].astype(MXU_DT), yc.astype(MXU_DT))
    m = None
    for n in range(3):
        zn = jnp.dot(ys[n], wbr_ref[n], preferred_element_type=F32)
        term = gate_ref[0, :, n * d:(n + 1) * d].astype(F32) * zn
        m = term if m is None else m + term
    out = jnp.dot(m.astype(MXU_DT), wout_ref[...], preferred_element_type=F32)
    o_ref[0] = x_ref[0] + g1_ref[0] * (_rms(out) * gpost_ref[...])


def _merge(x, g1, y5, att, o_f, o_b, hg, gate, w_glu, b_glu, hg_norm, w_branch, w_out, g_post):
    b, l, d = x.shape
    tm = min(512, l)
    w = BRANCH_W
    row = lambda bi, i: (bi, i, 0)
    vec = lambda bi, i: (bi, 0, 0)
    c2 = lambda bi, i: (0, 0)
    return pl.pallas_call(
        _merge_kernel,
        out_shape=jax.ShapeDtypeStruct((b, l, d), F32),
        grid=(b, l // tm),
        in_specs=[pl.BlockSpec((1, tm, d), row),
                  pl.BlockSpec((1, 1, d), vec),
                  pl.BlockSpec((1, tm, w), row),
                  pl.BlockSpec((1, tm, w), row),
                  pl.BlockSpec((1, tm, w), row),
                  pl.BlockSpec((1, tm, w), row),
                  pl.BlockSpec((1, tm, w), row),
                  pl.BlockSpec((1, tm, 3 * d), row),
                  pl.BlockSpec((w, w), c2),
                  pl.BlockSpec((1, w), c2),
                  pl.BlockSpec((1, w), c2),
                  pl.BlockSpec((3, w, d), lambda bi, i: (0, 0, 0)),
                  pl.BlockSpec((d, d), c2),
                  pl.BlockSpec((1, d), c2)],
        out_specs=pl.BlockSpec((1, tm, d), row),
        compiler_params=_cparams("parallel", "parallel"),
        name="merge_out",
    )(x, g1, y5, att, o_f, o_b, hg, gate, w_glu, b_glu, hg_norm, w_branch, w_out, g_post)


def _ffn_kernel(x_ref, xp_ref, xn_ref, sh_ref, sc_ref, g2_ref, gpre_ref, gpost_ref,
                wa_ref, wg_ref, cwa_ref, cwg_ref, cba_ref, cbg_ref, wd_ref, o_ref,
                h_ref, acc_ref, *, nrow, nj, tm):
    i = pl.program_id(1)
    j = pl.program_id(2)
    halo = SUBLANES

    @pl.when(j == 0)
    def _():
        def prep(xx):
            hh = _rms(xx) * gpre_ref[...]
            return hh * (1.0 + sc_ref[0]) + sh_ref[0]
        h_ref[halo:halo + tm] = prep(x_ref[0]).astype(h_ref.dtype)
        hp = jnp.where(i > 0, prep(xp_ref[0]), 0.0)
        hn = jnp.where(i < nrow - 1, prep(xn_ref[0]), 0.0)
        h_ref[0:halo] = hp.astype(h_ref.dtype)
        h_ref[halo + tm:halo + tm + halo] = hn.astype(h_ref.dtype)
        acc_ref[...] = jnp.zeros_like(acc_ref)

    hb = h_ref[...]
    rows = tm + 2 * halo

    def conv(w_ref, cw_ref, cb_ref):
        p = jnp.dot(hb, w_ref[...], preferred_element_type=F32)
        up = pltpu.roll(p, 1, 0)[halo:halo + tm]
        dn = pltpu.roll(p, rows - 1, 0)[halo:halo + tm]
        cw = cw_ref[...]
        return cw[0:1] * up + cw[1:2] * p[halo:halo + tm] + cw[2:3] * dn + cb_ref[...]

    a = conv(wa_ref, cwa_ref, cba_ref)
    g = conv(wg_ref, cwg_ref, cbg_ref)
    act = (a * jax.nn.sigmoid(a) * g).astype(MXU_DT)
    acc_ref[...] += jnp.dot(act, wd_ref[...], preferred_element_type=F32)

    @pl.when(j == nj - 1)
    def _():
        o_ref[0] = x_ref[0] + g2_ref[0] * (_rms(acc_ref[...]) * gpost_ref[...])


def _ffn(x, shift, scale, g2, g_pre, g_post, w_up, conv_w, conv_b, w_down):
    b, l, d = x.shape
    f = w_down.shape[0]
    tm = min(512, l)
    tn = 256
    nj = f // tn
    nrow = l // tm
    hb = tm // SUBLANES
    row = lambda bi, i, j: (bi, i, 0)
    vec = lambda bi, i, j: (bi, 0, 0)
    c2 = lambda bi, i, j: (0, 0)
    return pl.pallas_call(
        functools.partial(_ffn_kernel, nrow=nrow, nj=nj, tm=tm),
        out_shape=jax.ShapeDtypeStruct((b, l, d), F32),
        grid=(b, nrow, nj),
        in_specs=[pl.BlockSpec((1, tm, d), row),
                  pl.BlockSpec((1, SUBLANES, d), lambda bi, i, j: (bi, jnp.maximum(i * hb - 1, 0), 0)),
                  pl.BlockSpec((1, SUBLANES, d), lambda bi, i, j: (bi, jnp.minimum((i + 1) * hb, l // SUBLANES - 1), 0)),
                  pl.BlockSpec((1, 1, d), vec),
                  pl.BlockSpec((1, 1, d), vec),
                  pl.BlockSpec((1, 1, d), vec),
                  pl.BlockSpec((1, d), c2),
                  pl.BlockSpec((1, d), c2),
                  pl.BlockSpec((d, tn), lambda bi, i, j: (0, j)),
                  pl.BlockSpec((d, tn), lambda bi, i, j: (0, nj + j)),
                  pl.BlockSpec((CONV_W, tn), lambda bi, i, j: (0, j)),
                  pl.BlockSpec((CONV_W, tn), lambda bi, i, j: (0, nj + j)),
                  pl.BlockSpec((1, tn), lambda bi, i, j: (0, j)),
                  pl.BlockSpec((1, tn), lambda bi, i, j: (0, nj + j)),
                  pl.BlockSpec((tn, d), lambda bi, i, j: (j, 0))],
        out_specs=pl.BlockSpec((1, tm, d), row),
        scratch_shapes=[pltpu.VMEM((tm + 2 * SUBLANES, d), MXU_DT),
                        pltpu.VMEM((tm, d), F32)],
        compiler_params=_cparams("parallel", "parallel", "arbitrary"),
        name="conv_ffn",
    )(x, x, x, shift, scale, g2, g_pre, g_post, w_up, w_up, conv_w, conv_w, conv_b, conv_b, w_down)


def _rope_tables(l):
    rows = l // GRID_W
    row = jnp.repeat(jnp.arange(rows, dtype=F32), GRID_W)
    col = jnp.tile(jnp.arange(GRID_W, dtype=F32), rows)
    nf = HEAD_DIM // 4
    inv = ROPE_BASE ** (-jnp.arange(nf, dtype=F32) / nf)
    ang = jnp.concatenate([row[:, None] * inv, col[:, None] * inv], axis=-1)
    cos, sin = jnp.cos(ang), jnp.sin(ang)
    cos = jnp.tile(jnp.concatenate([cos, cos], axis=-1), (1, LANES // HEAD_DIM))
    sin = jnp.tile(jnp.concatenate([-sin, sin], axis=-1), (1, LANES // HEAD_DIM))
    return cos, sin


def _lower_bounds(logits):
    pr = jax.nn.softmax(logits.astype(F32), axis=0)
    cs = jnp.cumsum(pr, axis=0)
    return cs - cs[:1]


def kernel(x, c, ctx, c_ctx, w_mod, b_mod, norm_g, w_in, s5_lam_re, s5_lam_im, s5_log_dt, s5_b_re, s5_b_im, s5_c_re, s5_c_im, s5_d, s5_w_glu, s5_b_glu, att_sink, hg_lb_logits, hg_norm_g, w_branch, w_out, ffn_w_up, ffn_conv_w, ffn_conv_b, ffn_w_down):
    b, l, d = x.shape
    lc = ctx.shape[1]
    depth = w_in.shape[0]
    nc = c.shape[0]
    nrows = -(-(nc + 1) // SUBLANES) * SUBLANES
    cond = jnp.zeros((nrows, d), F32).at[:nc].set(c).at[nc].set(c_ctx)
    mods = _mod_all(cond, w_mod, b_mod)
    cos_l, sin_l = _rope_tables(l)
    cos_c = jnp.ones((lc, LANES), F32)
    sin_c = jnp.zeros((lc, LANES), F32)
    lb_all = _lower_bounds(hg_lb_logits)

    xl, xc = x, ctx
    for li in range(depth):
        with_ctx_out = li < depth - 1
        ml = mods[li, :nc].reshape(nc, 1, 6 * d)
        mc = jnp.broadcast_to(mods[li, nc].reshape(1, 1, 6 * d), (b, 1, 6 * d))
        sl = lambda m, k: m[:, :, k * d:(k + 1) * d]
        gains = norm_g[li].astype(F32)
        w_in_b = w_in[li].astype(MXU_DT)
        tables = _s5_tables(s5_lam_re[li], s5_lam_im[li], s5_log_dt[li], s5_b_re[li], s5_b_im[li],
                            s5_c_re[li], s5_c_im[li], s5_d[li])
        sink = att_sink[li].astype(F32)
        w_glu = s5_w_glu[li].astype(MXU_DT)
        b_glu = s5_b_glu[li].astype(F32).reshape(1, BRANCH_W)
        hgn = hg_norm_g[li].astype(F32).reshape(1, BRANCH_W)
        wbr = w_branch[li].astype(MXU_DT)
        wout = w_out[li].astype(MXU_DT)
        wup = ffn_w_up[li].astype(MXU_DT)
        wdn = ffn_w_down[li].astype(MXU_DT)
        cw = ffn_conv_w[li].astype(F32)
        cb = ffn_conv_b[li].astype(F32).reshape(1, -1)
        lb = lb_all[li]

        pc = _win(xc, sl(mc, 0), sl(mc, 1), gains[0:1], w_in_b, cos_c, sin_c)
        u_c, q_c, k_c, v_c, hq_c, ff_c, fb_c, hi_c, hg_c, gate_c = pc
        zero_h = jnp.zeros((b, S5_NQ, 1, 4 * S5_SW), F32)
        y5_c, h_ctx = _s5(u_c, tables, zero_h)
        zero_s = jnp.zeros((b, 2, HG_HEADS, HG_D, HG_D), F32)
        of_c, ob_c, s_ctx = _hgrn(hq_c, hi_c, ff_c, fb_c, lb, zero_s)

        pl_ = _win(xl, sl(ml, 0), sl(ml, 1), gains[0:1], w_in_b, cos_l, sin_l)
        u_l, q_l, k_l, v_l, hq_l, ff_l, fb_l, hi_l, hg_l, gate_l = pl_
        y5_l, _ = _s5(u_l, tables, h_ctx)
        att_l = _attn(q_l, k_l, v_l, k_c, v_c, sink)
        of_l, ob_l, _ = _hgrn(hq_l, hi_l, ff_l, fb_l, lb, s_ctx)
        xl = _merge(xl, sl(ml, 2), y5_l, att_l, of_l, ob_l, hg_l, gate_l,
                    w_glu, b_glu, hgn, wbr, wout, gains[1:2])
        xl = _ffn(xl, sl(ml, 3), sl(ml, 4), sl(ml, 5), gains[2:3], gains[3:4], wup, cw, cb, wdn)

        if with_ctx_out:
            att_c = _attn_ctx(q_c, k_c, v_c, sink)
            xc = _merge(xc, sl(mc, 2), y5_c, att_c, of_c, ob_c, hg_c, gate_c,
                        w_glu, b_glu, hgn, wbr, wout, gains[1:2])
            xc = _ffn(xc, sl(mc, 3), sl(mc, 4), sl(mc, 5), gains[2:3], gains[3:4], wup, cw, cb, wdn)
    return xl
```

```python
import functools
import math

import jax
import jax.numpy as jnp
from jax import lax
from jax.experimental import pallas as pl
from jax.experimental.pallas import tpu as pltpu

F32 = jnp.float32
MXU_DT = jnp.bfloat16
ACT_DT = jnp.bfloat16
EPS = 1e-6

BRANCH_W = 512
S5_GROUP = 16
S5_GROUPS = BRANCH_W // S5_GROUP
S5_STATE = 64
HEAD_DIM = 64
ATT_HEADS = 8
ATT_KV_HEADS = 2
ATT_GRP = ATT_HEADS // ATT_KV_HEADS
ATT_BLOCK = 128
GRID_W = 64
ROPE_BASE = 10000.0
HG_HEADS = 4
HG_D = 128
CONV_W = 3

LANES = 128
SUBLANES = 8
VMEM_LIMIT = 56 * 1024 * 1024

S5_T = 8
S5_GPT = LANES // S5_GROUP
S5_NQ = BRANCH_W // LANES
S5_SW = S5_GPT * S5_STATE
HG_CHUNK = 64
HG_MID = HG_CHUNK // 2


def _cparams(*sem):
    return pltpu.CompilerParams(dimension_semantics=sem, vmem_limit_bytes=VMEM_LIMIT)


def _rms(x):
    return x * lax.rsqrt(jnp.mean(x * x, axis=-1, keepdims=True) + EPS)


def _mod_kernel(a_ref, w_ref, b_ref, o_ref):
    a = a_ref[...]
    a = a * jax.nn.sigmoid(a)
    o_ref[0] = lax.dot_general(a, w_ref[0], (((1,), (0,)), ((), ())),
                               precision=lax.Precision.HIGHEST,
                               preferred_element_type=F32) + b_ref[0]


def _mod_all(cond, w_mod, b_mod):
    depth, d, n = w_mod.shape
    r = cond.shape[0]
    tn = 1536
    return pl.pallas_call(
        _mod_kernel,
        out_shape=jax.ShapeDtypeStruct((depth, r, n), F32),
        grid=(depth, n // tn),
        in_specs=[pl.BlockSpec((r, d), lambda l, j: (0, 0)),
                  pl.BlockSpec((1, d, tn), lambda l, j: (l, 0, j)),
                  pl.BlockSpec((1, 1, tn), lambda l, j: (l, 0, j))],
        out_specs=pl.BlockSpec((1, r, tn), lambda l, j: (l, 0, j)),
        compiler_params=_cparams("parallel", "parallel"),
        name="adaln_mod",
    )(cond, w_mod, b_mod.reshape(depth, 1, n))


_C_U = (0, 512)
_C_Q = (512, 1024)
_C_K = (1024, 1152)
_C_V = (1152, 1280)
_C_HQ = (1280, 1792)
_C_FF = (1792, 2304)
_C_FB = (2304, 2816)
_C_HI = (2816, 3328)
_C_HG = (3328, 3840)
_C_GATE = (3840, 6912)


def _rope(z, cos, sin):
    lane = lax.broadcasted_iota(jnp.int32, z.shape, 1)
    first = (lane & (HEAD_DIM // 2)) == 0
    partner = jnp.where(first, pltpu.roll(z, LANES - HEAD_DIM // 2, 1), pltpu.roll(z, HEAD_DIM // 2, 1))
    return z * cos + partner * sin


def _win_kernel(x_ref, sh_ref, sc_ref, g_ref, w_ref, cos_ref, sin_ref,
                u_ref, q_ref, k_ref, v_ref, hq_ref, ff_ref, fb_ref, hi_ref, hg_ref, gate_ref):
    x = x_ref[0]
    h = _rms(x) * g_ref[...]
    h = h * (1.0 + sc_ref[0]) + sh_ref[0]
    hb = h.astype(MXU_DT)

    def mm(lo, hi):
        return jnp.dot(hb, w_ref[:, lo:hi], preferred_element_type=F32)

    u_ref[0] = mm(*_C_U)
    cos = cos_ref[...]
    sin = sin_ref[...]
    for s in range(BRANCH_W // LANES):
        lo = _C_Q[0] + s * LANES
        z = _rope(mm(lo, lo + LANES), cos, sin) * (HEAD_DIM ** -0.5)
        q_ref[0, :, s * LANES:(s + 1) * LANES] = z.astype(q_ref.dtype)
    k_ref[0] = _rope(mm(*_C_K), cos, sin).astype(k_ref.dtype)
    v_ref[0] = mm(*_C_V).astype(v_ref.dtype)
    z = mm(*_C_HQ)
    hq_ref[0] = (z * jax.nn.sigmoid(z)).astype(hq_ref.dtype)
    ff_ref[0] = mm(*_C_FF)
    fb_ref[0] = mm(*_C_FB)
    hi_ref[0] = mm(*_C_HI).astype(hi_ref.dtype)
    hg_ref[0] = jax.nn.sigmoid(mm(*_C_HG)).astype(hg_ref.dtype)
    for s in range((_C_GATE[1] - _C_GATE[0]) // BRANCH_W):
        lo = _C_GATE[0] + s * BRANCH_W
        gate_ref[0, :, s * BRANCH_W:(s + 1) * BRANCH_W] = (
            jax.nn.sigmoid(mm(lo, lo + BRANCH_W)).astype(gate_ref.dtype))


def _win(x, shift, scale, gain, w, cos, sin):
    b, l, d = x.shape
    tm = min(512, l)
    n = w.shape[1]
    widths = [(512, F32), (512, ACT_DT), (128, ACT_DT), (128, ACT_DT), (512, ACT_DT),
              (512, F32), (512, F32), (512, ACT_DT), (512, ACT_DT), (3072, ACT_DT)]
    row = lambda bi, i: (bi, i, 0)
    vec = lambda bi, i: (bi, 0, 0)
    return pl.pallas_call(
        _win_kernel,
        out_shape=[jax.ShapeDtypeStruct((b, l, wd), dt) for wd, dt in widths],
        grid=(b, l // tm),
        in_specs=[pl.BlockSpec((1, tm, d), row),
                  pl.BlockSpec((1, 1, d), vec),
                  pl.BlockSpec((1, 1, d), vec),
                  pl.BlockSpec((1, d), lambda bi, i: (0, 0)),
                  pl.BlockSpec((d, n), lambda bi, i: (0, 0), pipeline_mode=pl.Buffered(1)),
                  pl.BlockSpec((tm, LANES), lambda bi, i: (i, 0)),
                  pl.BlockSpec((tm, LANES), lambda bi, i: (i, 0))],
        out_specs=[pl.BlockSpec((1, tm, wd), row) for wd, _ in widths],
        compiler_params=_cparams("parallel", "parallel"),
        name="in_proj",
    )(x, shift, scale, gain, w, cos, sin)


def _s5_tables(lam_re, lam_im, log_dt, b_re, b_im, c_re, c_im, d_skip):
    t_ = S5_T
    nq, gpt, sw = S5_NQ, S5_GPT, S5_SW
    hp = lax.Precision.HIGHEST
    lr = lam_re.astype(F32)
    li = lam_im.astype(F32)
    dt = jnp.exp(log_dt.astype(F32))[..., None]
    xr, xi = dt * lr, dt * li
    mag = jnp.exp(xr)
    ar, ai = mag * jnp.cos(xi), mag * jnp.sin(xi)
    den = lr * lr + li * li
    fr = ((ar - 1.0) * lr + ai * li) / den
    fi = (ai * lr - (ar - 1.0) * li) / den
    eye = jnp.eye(gpt, dtype=F32)

    def blockdiag(p):
        a, b = p.shape[1:]
        p4 = p.astype(F32).reshape(nq, gpt, a, b)
        return (p4[:, :, :, None, :] * eye[None, :, None, :, None]).reshape(nq, gpt * a, gpt * b)

    row = lambda a: a.reshape(a.shape[:-2] + (nq, 1, sw))
    col = lambda a: a.reshape(a.shape[:-2] + (nq, sw, 1))
    brt = blockdiag(jnp.swapaxes(b_re, 1, 2))
    bit = blockdiag(jnp.swapaxes(b_im, 1, 2))
    cxr = blockdiag(jnp.swapaxes(c_re, 1, 2))
    cxi = blockdiag(jnp.swapaxes(c_im, 1, 2))
    bbr = row(fr) * brt - row(fi) * bit
    bbi = row(fr) * bit + row(fi) * brt
    kk = jnp.arange(t_ + 1, dtype=F32)[:, None, None, None]
    pr = jnp.exp(kk * xr) * jnp.cos(kk * xi)
    pi = jnp.exp(kk * xr) * jnp.sin(kk * xi)
    wr = row(pr) * bbr - row(pi) * bbi
    wi = row(pr) * bbi + row(pi) * bbr
    kern = (jnp.einsum('kdqim,qmo->kdqio', wr[:t_], cxr, precision=hp)
            - jnp.einsum('kdqim,qmo->kdqio', wi[:t_], cxi, precision=hp))
    s_idx = jnp.arange(t_)[:, None]
    t_idx = jnp.arange(t_)[None, :]
    sel = lambda m: m[..., None, None, None]
    kst = (jnp.where(sel(t_idx >= s_idx), kern[jnp.clip(t_idx - s_idx, 0, t_ - 1), 0], 0.0)
           + jnp.where(sel(s_idx >= t_idx), kern[jnp.clip(s_idx - t_idx, 0, t_ - 1), 1], 0.0)
           + jnp.where(sel(s_idx == t_idx),
                       jnp.eye(LANES, dtype=F32) * d_skip.astype(F32).reshape(nq, 1, LANES), 0.0))
    toe = jnp.transpose(kst, (2, 0, 3, 1, 4)).reshape(nq, t_ * LANES, t_ * LANES)

    q4 = jnp.stack([jnp.stack([wr[:t_, 0][::-1], wi[:t_, 0][::-1]], 0),
                    jnp.stack([wr[:t_, 1], wi[:t_, 1]], 0)], 0)
    qm = jnp.transpose(q4, (3, 2, 4, 0, 1, 5)).reshape(nq, t_ * LANES, 4 * sw)

    psr = col(jnp.stack([pr[1:, 0], pr[1:, 1][::-1]], 0))
    psi = col(jnp.stack([pi[1:, 0], pi[1:, 1][::-1]], 0))
    p4 = jnp.stack([cxr * psr - cxi * psi, -cxr * psi - cxi * psr], 1)
    pm = jnp.transpose(p4, (3, 0, 1, 4, 2, 5)).reshape(nq, 4 * sw, t_ * LANES)

    dsel = jnp.stack([pr[t_], pi[t_]], 1).reshape(2, 2, nq, sw)
    dec = jnp.transpose(dsel, (2, 0, 1, 3)).reshape(nq, 1, 4 * sw)
    return toe.astype(MXU_DT), qm.astype(MXU_DT), pm.astype(MXU_DT), dec


def _s5_kernel(u_ref, toe_ref, qm_ref, pm_ref, dec_ref, h0_ref, y_ref, hend_ref,
               z_ref, s_ref, hin_ref, *, nj):
    t_, sw = S5_T, S5_SW
    for t in range(t_):
        z_ref[:, t * LANES:(t + 1) * LANES] = u_ref[0, pl.ds(t, nj, stride=t_), :].astype(z_ref.dtype)
    s_ref[...] = jnp.dot(z_ref[...], qm_ref[0], preferred_element_type=F32)
    dec = dec_ref[0]
    dfr, dfi, dbr, dbi = (dec[:, i * sw:(i + 1) * sw] for i in range(4))
    h0 = h0_ref[0, 0]

    def step(j, carry):
        fr, fi, br, bi = carry
        jb = nj - 1 - j
        hin_ref[pl.ds(j, 1), 0 * sw:1 * sw] = fr
        hin_ref[pl.ds(j, 1), 1 * sw:2 * sw] = fi
        hin_ref[pl.ds(jb, 1), 2 * sw:3 * sw] = br
        hin_ref[pl.ds(jb, 1), 3 * sw:4 * sw] = bi
        sfr = s_ref[pl.ds(j, 1), 0 * sw:1 * sw]
        sfi = s_ref[pl.ds(j, 1), 1 * sw:2 * sw]
        sbr = s_ref[pl.ds(jb, 1), 2 * sw:3 * sw]
        sbi = s_ref[pl.ds(jb, 1), 3 * sw:4 * sw]
        return (dfr * fr - dfi * fi + sfr, dfr * fi + dfi * fr + sfi,
                dbr * br - dbi * bi + sbr, dbr * bi + dbi * br + sbi)

    fin = lax.fori_loop(0, nj, step, tuple(h0[:, i * sw:(i + 1) * sw] for i in range(4)))
    for i in range(4):
        hend_ref[0, 0, :, i * sw:(i + 1) * sw] = fin[i]
    y = (jnp.dot(z_ref[...], toe_ref[0], preferred_element_type=F32)
         + jnp.dot(hin_ref[...].astype(MXU_DT), pm_ref[0], preferred_element_type=F32))
    for t in range(t_):
        y_ref[0, pl.ds(t, nj, stride=t_), :] = y[:, t * LANES:(t + 1) * LANES]


def _s5(u, tables, h0):
    toe, qm, pm, dec = tables
    b, l, _ = u.shape
    nj = l // S5_T
    kw = S5_T * LANES
    sw4 = 4 * S5_SW
    wmap = lambda q, bi: (q, 0, 0)
    return pl.pallas_call(
        functools.partial(_s5_kernel, nj=nj),
        out_shape=[jax.ShapeDtypeStruct((b, l, BRANCH_W), F32),
                   jax.ShapeDtypeStruct((b, S5_NQ, 1, sw4), F32)],
        grid=(S5_NQ, b),
        in_specs=[pl.BlockSpec((1, l, LANES), lambda q, bi: (bi, 0, q)),
                  pl.BlockSpec((1, kw, kw), wmap),
                  pl.BlockSpec((1, kw, sw4), wmap),
                  pl.BlockSpec((1, sw4, kw), wmap),
                  pl.BlockSpec((1, 1, sw4), wmap),
                  pl.BlockSpec((1, 1, 1, sw4), lambda q, bi: (bi, q, 0, 0))],
        out_specs=[pl.BlockSpec((1, l, LANES), lambda q, bi: (bi, 0, q)),
                   pl.BlockSpec((1, 1, 1, sw4), lambda q, bi: (bi, q, 0, 0))],
        scratch_shapes=[pltpu.VMEM((nj, kw), MXU_DT),
                        pltpu.VMEM((nj, sw4), F32),
                        pltpu.VMEM((nj, sw4), F32)],
        compiler_params=_cparams("parallel", "parallel"),
        name="s5_mix",
    )(u, toe, qm, pm, dec, h0)


def _softmax_pv(s, sink_col, v):
    m = jnp.maximum(jnp.max(s, axis=-1, keepdims=True), sink_col)
    p = jnp.exp(s - m)
    den = jnp.sum(p, axis=-1, keepdims=True) + jnp.exp(sink_col - m)
    o = jnp.dot(p.astype(MXU_DT), v, preferred_element_type=F32)
    return o / den


def _attn_kernel(sink_ref, q_ref, kp_ref, kc_ref, kn_ref, vp_ref, vc_ref, vn_ref, kx_ref, vx_ref,
                 o_ref, *, nb, lc):
    i = pl.program_id(1)
    blk = ATT_BLOCK
    neg = jnp.float32(-jnp.inf)
    qi = lax.broadcasted_iota(jnp.int32, (blk, blk), 0)
    kj = lax.broadcasted_iota(jnp.int32, (blk, blk), 1)
    bias_p = jnp.where(i > 0, jnp.where(kj >= qi, 0.0, neg), neg)
    bias_n = jnp.where(i < nb - 1, jnp.where(kj <= qi, 0.0, neg), neg)
    bias = jnp.concatenate([jnp.zeros((blk, lc), F32), bias_p, jnp.zeros((blk, blk), F32), bias_n], axis=1)
    bias = jnp.concatenate([bias] * ATT_GRP, axis=0)
    outs = []
    for hk in range(ATT_KV_HEADS):
        ks = slice(hk * HEAD_DIM, (hk + 1) * HEAD_DIM)
        kcat = jnp.concatenate([kx_ref[0, :, ks], kp_ref[0, :, ks], kc_ref[0, :, ks], kn_ref[0, :, ks]], axis=0)
        vcat = jnp.concatenate([vx_ref[0, :, ks], vp_ref[0, :, ks], vc_ref[0, :, ks], vn_ref[0, :, ks]], axis=0)
        qs = jnp.concatenate(
            [q_ref[0, :, (hk * ATT_GRP + g) * HEAD_DIM:(hk * ATT_GRP + g + 1) * HEAD_DIM] for g in range(ATT_GRP)],
            axis=0)
        sink_col = jnp.concatenate(
            [jnp.full((blk, 1), sink_ref[hk * ATT_GRP + g], F32) for g in range(ATT_GRP)], axis=0)
        s = lax.dot_general(qs, kcat, (((1,), (1,)), ((), ())), preferred_element_type=F32) + bias
        o = _softmax_pv(s, sink_col, vcat)
        outs.extend(o[g * blk:(g + 1) * blk] for g in range(ATT_GRP))
    o_ref[0] = jnp.concatenate(outs, axis=1).astype(o_ref.dtype)


def _attn(q, k, v, kx, vx, sink):
    b, l, _ = q.shape
    lc = kx.shape[1]
    nb = l // ATT_BLOCK
    kvw = ATT_KV_HEADS * HEAD_DIM
    cur = lambda bi, i: (bi, i, 0)
    prev = lambda bi, i: (bi, jnp.maximum(i - 1, 0), 0)
    nxt = lambda bi, i: (bi, jnp.minimum(i + 1, nb - 1), 0)
    ctx = lambda bi, i: (bi, 0, 0)
    kvspec = lambda m: pl.BlockSpec((1, ATT_BLOCK, kvw), m)
    return pl.pallas_call(
        functools.partial(_attn_kernel, nb=nb, lc=lc),
        out_shape=jax.ShapeDtypeStruct((b, l, BRANCH_W), ACT_DT),
        grid=(b, nb),
        in_specs=[pl.BlockSpec(memory_space=pltpu.SMEM),
                  pl.BlockSpec((1, ATT_BLOCK, BRANCH_W), cur),
                  kvspec(prev), kvspec(cur), kvspec(nxt),
                  kvspec(prev), kvspec(cur), kvspec(nxt),
                  pl.BlockSpec((1, lc, kvw), ctx),
                  pl.BlockSpec((1, lc, kvw), ctx)],
        out_specs=pl.BlockSpec((1, ATT_BLOCK, BRANCH_W), cur),
        compiler_params=_cparams("parallel", "parallel"),
        name="window_attn",
    )(sink, q, k, k, k, v, v, v, kx, vx)


def _attn_ctx_kernel(sink_ref, q_ref, k_ref, v_ref, o_ref, *, lc):
    outs = []
    for hk in range(ATT_KV_HEADS):
        ks = slice(hk * HEAD_DIM, (hk + 1) * HEAD_DIM)
        kk = k_ref[0, :, ks]
        vv = v_ref[0, :, ks]
        qs = jnp.concatenate(
            [q_ref[0, :, (hk * ATT_GRP + g) * HEAD_DIM:(hk * ATT_GRP + g + 1) * HEAD_DIM] for g in range(ATT_GRP)],
            axis=0)
        sink_col = jnp.concatenate(
            [jnp.full((lc, 1), sink_ref[hk * ATT_GRP + g], F32) for g in range(ATT_GRP)], axis=0)
        s = lax.dot_general(qs, kk, (((1,), (1,)), ((), ())), preferred_element_type=F32)
        o = _softmax_pv(s, sink_col, vv)
        outs.extend(o[g * lc:(g + 1) * lc] for g in range(ATT_GRP))
    o_ref[0] = jnp.concatenate(outs, axis=1).astype(o_ref.dtype)


def _attn_ctx(q, k, v, sink):
    b, lc, _ = q.shape
    kvw = ATT_KV_HEADS * HEAD_DIM
    full = lambda bi: (bi, 0, 0)
    return pl.pallas_call(
        functools.partial(_attn_ctx_kernel, lc=lc),
        out_shape=jax.ShapeDtypeStruct((b, lc, BRANCH_W), ACT_DT),
        grid=(b,),
        in_specs=[pl.BlockSpec(memory_space=pltpu.SMEM),
                  pl.BlockSpec((1, lc, BRANCH_W), full),
                  pl.BlockSpec((1, lc, kvw), full),
                  pl.BlockSpec((1, lc, kvw), full)],
        out_specs=pl.BlockSpec((1, lc, BRANCH_W), full),
        compiler_params=_cparams("parallel"),
        name="ctx_attn",
    )(sink, q, k, v)


def _split3(x):
    a = x.astype(jnp.bfloat16)
    r = x - a.astype(F32)
    b = r.astype(jnp.bfloat16)
    c = (r - b.astype(F32)).astype(jnp.bfloat16)
    return a, b, c


def _hg_chunk(q, z, v, lb, tri, st_ref, d, reverse):
    c = HG_CHUNK
    s = jax.nn.sigmoid(z)
    f = lb + (1.0 - lb) * s
    logf = jnp.log(f)
    kf = (1.0 - lb) * (1.0 - s)
    cum = sum(jnp.dot(tri, part, preferred_element_type=F32) for part in _split3(logf))
    end_row = 0 if reverse else c - 1
    mid_row = HG_MID if reverse else HG_MID - 1
    cum_end = cum[end_row:end_row + 1]
    cum_mid = cum[mid_row:mid_row + 1]
    rel = cum - cum_mid
    qf = q.astype(F32)
    q_in = (qf * jnp.exp(rel)).astype(MXU_DT)
    k_in = (kf * jnp.exp(-rel)).astype(MXU_DT)
    q_st = (qf * jnp.exp(cum)).astype(MXU_DT)
    k_st = (kf * jnp.exp(cum_end - cum)).astype(MXU_DT)
    vb = v.astype(MXU_DT)
    dec = jnp.exp(cum_end)
    outs = []
    for h in range(HG_HEADS):
        hs = slice(h * HG_D, (h + 1) * HG_D)
        att = lax.dot_general(q_in[:, hs], k_in[:, hs], (((1,), (1,)), ((), ())), preferred_element_type=F32)
        att = jnp.where(tri > 0, att, 0.0).astype(MXU_DT)
        st = st_ref[d, h]
        o = jnp.dot(att, vb[:, hs], preferred_element_type=F32)
        o = o + lax.dot_general(q_st[:, hs], st.astype(MXU_DT), (((1,), (1,)), ((), ())),
                                preferred_element_type=F32)
        st_ref[d, h] = st * dec[:, hs] + lax.dot_general(vb[:, hs], k_st[:, hs], (((0,), (0,)), ((), ())),
                                                         preferred_element_type=F32)
        outs.append(o)
    return jnp.concatenate(outs, axis=1)


def _hgrn_kernel(qf_ref, qb_ref, vf_ref, vb_ref, ff_ref, fb_ref, lb_ref, s0_ref,
                 of_ref, ob_ref, send_ref, st_ref, *, nblk, cpb):
    j = pl.program_id(1)
    c = HG_CHUNK

    @pl.when(j == 0)
    def _():
        st_ref[...] = s0_ref[0]

    r = lax.broadcasted_iota(jnp.int32, (c, c), 0)
    col = lax.broadcasted_iota(jnp.int32, (c, c), 1)
    tri_f = (r >= col).astype(MXU_DT)
    tri_b = (r <= col).astype(MXU_DT)
    lbf = lb_ref[0:1]
    lbb = lb_ref[1:2]
    for ci in range(cpb):
        fs = slice(ci * c, (ci + 1) * c)
        bs = slice((cpb - 1 - ci) * c, (cpb - ci) * c)
        of_ref[0, fs] = _hg_chunk(qf_ref[0, fs], ff_ref[0, fs], vf_ref[0, fs], lbf, tri_f, st_ref, 0,
                                  False).astype(of_ref.dtype)
        ob_ref[0, bs] = _hg_chunk(qb_ref[0, bs], fb_ref[0, bs], vb_ref[0, bs], lbb, tri_b, st_ref, 1,
                                  True).astype(ob_ref.dtype)

    @pl.when(j == nblk - 1)
    def _():
        send_ref[0] = st_ref[...]


def _hgrn(hq, hi, ff, fb, lb, s0):
    b, l, w = hq.shape
    tb = min(256, l)
    nblk = l // tb
    fw = lambda bi, j: (bi, j, 0)
    bw = lambda bi, j: (bi, nblk - 1 - j, 0)
    st = lambda bi, j: (bi, 0, 0, 0, 0)
    blk = lambda m: pl.BlockSpec((1, tb, w), m)
    return pl.pallas_call(
        functools.partial(_hgrn_kernel, nblk=nblk, cpb=tb // HG_CHUNK),
        out_shape=[jax.ShapeDtypeStruct((b, l, w), ACT_DT),
                   jax.ShapeDtypeStruct((b, l, w), ACT_DT),
                   jax.ShapeDtypeStruct(s0.shape, F32)],
        grid=(b, nblk),
        in_specs=[blk(fw), blk(bw), blk(fw), blk(bw), blk(fw), blk(bw),
                  pl.BlockSpec((2, w), lambda bi, j: (0, 0)),
                  pl.BlockSpec((1,) + s0.shape[1:], st)],
        out_specs=[blk(fw), blk(bw), pl.BlockSpec((1,) + s0.shape[1:], st)],
        scratch_shapes=[pltpu.VMEM(s0.shape[1:], F32)],
        compiler_params=_cparams("parallel", "arbitrary"),
        name="hgrn2_scan",
    )(hq, hq, hi, hi, ff, fb, lb, s0)


def _merge_kernel(x_ref, g1_ref, y5_ref, att_ref, of_ref, ob_ref, hg_ref, gate_ref,
                  wglu_ref, bglu_ref, hgn_ref, wbr_ref, wout_ref, gpost_ref, o_ref):
    d = x_ref.shape[-1]
    ge = jax.nn.gelu(y5_ref[0])
    ya = ge * jax.nn.sigmoid(jnp.dot(ge.astype(MXU_DT), wglu_ref[...], preferred_element_type=F32)
                             + bglu_ref[...])
    o = of_ref[0].astype(F32) + ob_ref[0].astype(F32)
    o = jnp.concatenate([_rms(o[:, h * HG_D:(h + 1) * HG_D]) for h in range(HG_HEADS)], axis=1)
    yc = o * hgn_ref[...] * hg_ref[0].astype(F32)
    ys = (ya.astype(MXU_DT), att_ref[0].astype(MXU_DT), yc.astype(MXU_DT))
    m = None
    for n in range(3):
        zn = jnp.dot(ys[n], wbr_ref[n], preferred_element_type=F32)
        term = gate_ref[0, :, n * d:(n + 1) * d].astype(F32) * zn
        m = term if m is None else m + term
    out = jnp.dot(m.astype(MXU_DT), wout_ref[...], preferred_element_type=F32)
    o_ref[0] = x_ref[0] + g1_ref[0] * (_rms(out) * gpost_ref[...])


def _merge(x, g1, y5, att, o_f, o_b, hg, gate, w_glu, b_glu, hg_norm, w_branch, w_out, g_post):
    b, l, d = x.shape
    tm = min(512, l)
    w = BRANCH_W
    row = lambda bi, i: (bi, i, 0)
    vec = lambda bi, i: (bi, 0, 0)
    c2 = lambda bi, i: (0, 0)
    return pl.pallas_call(
        _merge_kernel,
        out_shape=jax.ShapeDtypeStruct((b, l, d), F32),
        grid=(b, l // tm),
        in_specs=[pl.BlockSpec((1, tm, d), row),
                  pl.BlockSpec((1, 1, d), vec),
                  pl.BlockSpec((1, tm, w), row),
                  pl.BlockSpec((1, tm, w), row),
                  pl.BlockSpec((1, tm, w), row),
                  pl.BlockSpec((1, tm, w), row),
                  pl.BlockSpec((1, tm, w), row),
                  pl.BlockSpec((1, tm, 3 * d), row),
                  pl.BlockSpec((w, w), c2),
                  pl.BlockSpec((1, w), c2),
                  pl.BlockSpec((1, w), c2),
                  pl.BlockSpec((3, w, d), lambda bi, i: (0, 0, 0)),
                  pl.BlockSpec((d, d), c2),
                  pl.BlockSpec((1, d), c2)],
        out_specs=pl.BlockSpec((1, tm, d), row),
        compiler_params=_cparams("parallel", "parallel"),
        name="merge_out",
    )(x, g1, y5, att, o_f, o_b, hg, gate, w_glu, b_glu, hg_norm, w_branch, w_out, g_post)


def _ffn_kernel(x_ref, xp_ref, xn_ref, sh_ref, sc_ref, g2_ref, gpre_ref, gpost_ref,
                wup_ref, cw_ref, cb_ref, wd_ref, o_ref, h_ref, act_ref, *, nrow, tm, tn):
    i = pl.program_id(1)
    halo = SUBLANES
    f = wd_ref.shape[0]
    rows = tm + 2 * halo

    def prep(xx):
        hh = _rms(xx) * gpre_ref[...]
        return hh * (1.0 + sc_ref[0]) + sh_ref[0]

    h_ref[halo:halo + tm] = prep(x_ref[0]).astype(h_ref.dtype)
    h_ref[0:halo] = jnp.where(i > 0, prep(xp_ref[0]), 0.0).astype(h_ref.dtype)
    h_ref[halo + tm:rows] = jnp.where(i < nrow - 1, prep(xn_ref[0]), 0.0).astype(h_ref.dtype)

    def conv(c0):
        p = jnp.dot(h_ref[...], wup_ref[:, c0:c0 + tn], preferred_element_type=F32)
        up = pltpu.roll(p, 1, 0)[halo:halo + tm]
        dn = pltpu.roll(p, rows - 1, 0)[halo:halo + tm]
        cw = cw_ref[:, c0:c0 + tn]
        return cw[0:1] * up + cw[1:2] * p[halo:halo + tm] + cw[2:3] * dn + cb_ref[:, c0:c0 + tn]

    for jt in range(f // tn):
        a = conv(jt * tn)
        g = conv(f + jt * tn)
        act_ref[:, jt * tn:(jt + 1) * tn] = (a * jax.nn.sigmoid(a) * g).astype(act_ref.dtype)
    out = jnp.dot(act_ref[...], wd_ref[...], preferred_element_type=F32)
    o_ref[0] = x_ref[0] + g2_ref[0] * (_rms(out) * gpost_ref[...])


def _ffn(x, shift, scale, g2, g_pre, g_post, w_up, conv_w, conv_b, w_down):
    b, l, d = x.shape
    f = w_down.shape[0]
    tm = min(512, l)
    tn = 256
    nrow = l // tm
    hb = tm // SUBLANES
    row = lambda bi, i: (bi, i, 0)
    vec = lambda bi, i: (bi, 0, 0)
    c2 = lambda bi, i: (0, 0)
    resident = lambda shape: pl.BlockSpec(shape, c2, pipeline_mode=pl.Buffered(1))
    return pl.pallas_call(
        functools.partial(_ffn_kernel, nrow=nrow, tm=tm, tn=tn),
        out_shape=jax.ShapeDtypeStruct((b, l, d), F32),
        grid=(b, nrow),
        in_specs=[pl.BlockSpec((1, tm, d), row),
                  pl.BlockSpec((1, SUBLANES, d), lambda bi, i: (bi, jnp.maximum(i * hb - 1, 0), 0)),
                  pl.BlockSpec((1, SUBLANES, d), lambda bi, i: (bi, jnp.minimum((i + 1) * hb, l // SUBLANES - 1), 0)),
                  pl.BlockSpec((1, 1, d), vec),
                  pl.BlockSpec((1, 1, d), vec),
                  pl.BlockSpec((1, 1, d), vec),
                  pl.BlockSpec((1, d), c2),
                  pl.BlockSpec((1, d), c2),
                  resident((d, 2 * f)),
                  resident((CONV_W, 2 * f)),
                  resident((1, 2 * f)),
                  resident((f, d))],
        out_specs=pl.BlockSpec((1, tm, d), row),
        scratch_shapes=[pltpu.VMEM((tm + 2 * SUBLANES, d), MXU_DT),
                        pltpu.VMEM((tm, f), MXU_DT)],
        compiler_params=_cparams("parallel", "parallel"),
        name="conv_ffn",
    )(x, x, x, shift, scale, g2, g_pre, g_post, w_up, conv_w, conv_b, w_down)


def _rope_tables(l):
    rows = l // GRID_W
    row = jnp.repeat(jnp.arange(rows, dtype=F32), GRID_W)
    col = jnp.tile(jnp.arange(GRID_W, dtype=F32), rows)
    nf = HEAD_DIM // 4
    inv = ROPE_BASE ** (-jnp.arange(nf, dtype=F32) / nf)
    ang = jnp.concatenate([row[:, None] * inv, col[:, None] * inv], axis=-1)
    cos, sin = jnp.cos(ang), jnp.sin(ang)
    cos = jnp.tile(jnp.concatenate([cos, cos], axis=-1), (1, LANES // HEAD_DIM))
    sin = jnp.tile(jnp.concatenate([-sin, sin], axis=-1), (1, LANES // HEAD_DIM))
    return cos, sin


def _lower_bounds(logits):
    pr = jax.nn.softmax(logits.astype(F32), axis=0)
    cs = jnp.cumsum(pr, axis=0)
    return cs - cs[:1]


def kernel(x, c, ctx, c_ctx, w_mod, b_mod, norm_g, w_in, s5_lam_re, s5_lam_im, s5_log_dt, s5_b_re, s5_b_im, s5_c_re, s5_c_im, s5_d, s5_w_glu, s5_b_glu, att_sink, hg_lb_logits, hg_norm_g, w_branch, w_out, ffn_w_up, ffn_conv_w, ffn_conv_b, ffn_w_down):
    b, l, d = x.shape
    lc = ctx.shape[1]
    depth = w_in.shape[0]
    nc = c.shape[0]
    nrows = -(-(nc + 1) // SUBLANES) * SUBLANES
    cond = jnp.zeros((nrows, d), F32).at[:nc].set(c).at[nc].set(c_ctx)
    mods = _mod_all(cond, w_mod, b_mod)
    cos_l, sin_l = _rope_tables(l)
    cos_c = jnp.ones((lc, LANES), F32)
    sin_c = jnp.zeros((lc, LANES), F32)
    lb_all = _lower_bounds(hg_lb_logits)
    tables_all = jax.vmap(_s5_tables)(s5_lam_re, s5_lam_im, s5_log_dt, s5_b_re, s5_b_im,
                                      s5_c_re, s5_c_im, s5_d)

    xl, xc = x, ctx
    for li in range(depth):
        with_ctx_out = li < depth - 1
        ml = mods[li, :nc].reshape(nc, 1, 6 * d)
        mc = jnp.broadcast_to(mods[li, nc].reshape(1, 1, 6 * d), (b, 1, 6 * d))
        sl = lambda m, k: m[:, :, k * d:(k + 1) * d]
        gains = norm_g[li].astype(F32)
        w_in_b = w_in[li].astype(MXU_DT)
        tables = tuple(t[li] for t in tables_all)
        sink = att_sink[li].astype(F32)
        w_glu = s5_w_glu[li].astype(MXU_DT)
        b_glu = s5_b_glu[li].astype(F32).reshape(1, BRANCH_W)
        hgn = hg_norm_g[li].astype(F32).reshape(1, BRANCH_W)
        wbr = w_branch[li].astype(MXU_DT)
        wout = w_out[li].astype(MXU_DT)
        wup = ffn_w_up[li].astype(MXU_DT)
        wdn = ffn_w_down[li].astype(MXU_DT)
        cw = ffn_conv_w[li].astype(F32)
        cb = ffn_conv_b[li].astype(F32).reshape(1, -1)
        lb = lb_all[li]

        pc = _win(xc, sl(mc, 0), sl(mc, 1), gains[0:1], w_in_b, cos_c, sin_c)
        u_c, q_c, k_c, v_c, hq_c, ff_c, fb_c, hi_c, hg_c, gate_c = pc
        zero_h = jnp.zeros((b, S5_NQ, 1, 4 * S5_SW), F32)
        y5_c, h_ctx = _s5(u_c, tables, zero_h)
        zero_s = jnp.zeros((b, 2, HG_HEADS, HG_D, HG_D), F32)
        of_c, ob_c, s_ctx = _hgrn(hq_c, hi_c, ff_c, fb_c, lb, zero_s)

        pl_ = _win(xl, sl(ml, 0), sl(ml, 1), gains[0:1], w_in_b, cos_l, sin_l)
        u_l, q_l, k_l, v_l, hq_l, ff_l, fb_l, hi_l, hg_l, gate_l = pl_
        y5_l, _ = _s5(u_l, tables, h_ctx)
        att_l = _attn(q_l, k_l, v_l, k_c, v_c, sink)
        of_l, ob_l, _ = _hgrn(hq_l, hi_l, ff_l, fb_l, lb, s_ctx)
        xl = _merge(xl, sl(ml, 2), y5_l, att_l, of_l, ob_l, hg_l, gate_l,
                    w_glu, b_glu, hgn, wbr, wout, gains[1:2])
        xl = _ffn(xl, sl(ml, 3), sl(ml, 4), sl(ml, 5), gains[2:3], gains[3:4], wup, cw, cb, wdn)

        if with_ctx_out:
            att_c = _attn_ctx(q_c, k_c, v_c, sink)
            xc = _merge(xc, sl(mc, 2), y5_c, att_c, of_c, ob_c, hg_c, gate_c,
                        w_glu, b_glu, hgn, wbr, wout, gains[1:2])
            xc = _ffn(xc, sl(mc, 3), sl(mc, 4), sl(mc, 5), gains[2:3], gains[3:4], wup, cw, cb, wdn)
    return xl
```

```python
import functools
import math

import jax
import jax.numpy as jnp
from jax import lax
from jax.experimental import pallas as pl
from jax.experimental.pallas import tpu as pltpu

F32 = jnp.float32
MXU_DT = jnp.bfloat16
ACT_DT = jnp.bfloat16
EPS = 1e-6

BRANCH_W = 512
S5_GROUP = 16
S5_GROUPS = BRANCH_W // S5_GROUP
S5_STATE = 64
HEAD_DIM = 64
ATT_HEADS = 8
ATT_KV_HEADS = 2
ATT_GRP = ATT_HEADS // ATT_KV_HEADS
ATT_BLOCK = 128
GRID_W = 64
ROPE_BASE = 10000.0
HG_HEADS = 4
HG_D = 128
CONV_W = 3

LANES = 128
SUBLANES = 8
VMEM_LIMIT = 56 * 1024 * 1024

S5_T = 8
S5_GPT = LANES // S5_GROUP
S5_NQ = BRANCH_W // LANES
S5_SW = S5_GPT * S5_STATE
HG_CHUNK = 64
HG_MID = HG_CHUNK // 2


def _cparams(*sem):
    return pltpu.CompilerParams(dimension_semantics=sem, vmem_limit_bytes=VMEM_LIMIT)


def _rms(x):
    return x * lax.rsqrt(jnp.mean(x * x, axis=-1, keepdims=True) + EPS)


def _mod_kernel(a_ref, w_ref, b_ref, o_ref):
    a = a_ref[...]
    a = a * jax.nn.sigmoid(a)
    o_ref[0] = lax.dot_general(a, w_ref[0], (((1,), (0,)), ((), ())),
                               precision=lax.Precision.HIGHEST,
                               preferred_element_type=F32) + b_ref[0]


def _mod_all(cond, w_mod, b_mod):
    depth, d, n = w_mod.shape
    r = cond.shape[0]
    tn = 1536
    return pl.pallas_call(
        _mod_kernel,
        out_shape=jax.ShapeDtypeStruct((depth, r, n), F32),
        grid=(depth, n // tn),
        in_specs=[pl.BlockSpec((r, d), lambda l, j: (0, 0)),
                  pl.BlockSpec((1, d, tn), lambda l, j: (l, 0, j)),
                  pl.BlockSpec((1, 1, tn), lambda l, j: (l, 0, j))],
        out_specs=pl.BlockSpec((1, r, tn), lambda l, j: (l, 0, j)),
        compiler_params=_cparams("parallel", "parallel"),
        name="adaln_mod",
    )(cond, w_mod, b_mod.reshape(depth, 1, n))


_C_U = (0, 512)
_C_Q = (512, 1024)
_C_K = (1024, 1280)
_C_V = (1280, 1536)
_C_HQ = (1536, 2048)
_C_FF = (2048, 2560)
_C_FB = (2560, 3072)
_C_HI = (3072, 3584)
_C_HG = (3584, 4096)
_C_GATE = (4096, 7168)
KV_W = 2 * ATT_KV_HEADS * HEAD_DIM


def _rope(z, cos, sin):
    lane = lax.broadcasted_iota(jnp.int32, z.shape, 1)
    first = (lane & (HEAD_DIM // 2)) == 0
    partner = jnp.where(first, pltpu.roll(z, LANES - HEAD_DIM // 2, 1), pltpu.roll(z, HEAD_DIM // 2, 1))
    return z * cos + partner * sin


def _win_kernel(x_ref, sh_ref, sc_ref, g_ref, w_ref, cos_ref, sin_ref,
                u_ref, q_ref, k_ref, v_ref, hq_ref, ff_ref, fb_ref, hi_ref, hg_ref, gate_ref):
    x = x_ref[0]
    h = _rms(x) * g_ref[...]
    h = h * (1.0 + sc_ref[0]) + sh_ref[0]
    hb = h.astype(MXU_DT)

    def mm(lo, hi):
        return jnp.dot(hb, w_ref[:, lo:hi], preferred_element_type=F32)

    u_ref[0] = mm(*_C_U)
    cos = cos_ref[...]
    sin = sin_ref[...]
    zq = mm(*_C_Q)
    for s in range(BRANCH_W // LANES):
        z = _rope(zq[:, s * LANES:(s + 1) * LANES], cos, sin) * (HEAD_DIM ** -0.5)
        q_ref[0, :, s * LANES:(s + 1) * LANES] = z.astype(q_ref.dtype)
    zkv = mm(_C_K[0], _C_V[1])
    for s in range(KV_W // LANES):
        k_ref[0, :, s * LANES:(s + 1) * LANES] = _rope(zkv[:, s * LANES:(s + 1) * LANES], cos, sin).astype(k_ref.dtype)
    v_ref[0] = zkv[:, KV_W:].astype(v_ref.dtype)
    z = mm(*_C_HQ)
    hq_ref[0] = (z * jax.nn.sigmoid(z)).astype(hq_ref.dtype)
    ff_ref[0] = mm(*_C_FF)
    fb_ref[0] = mm(*_C_FB)
    hi_ref[0] = mm(*_C_HI).astype(hi_ref.dtype)
    hg_ref[0] = jax.nn.sigmoid(mm(*_C_HG)).astype(hg_ref.dtype)
    for s in range((_C_GATE[1] - _C_GATE[0]) // BRANCH_W):
        lo = _C_GATE[0] + s * BRANCH_W
        gate_ref[0, :, s * BRANCH_W:(s + 1) * BRANCH_W] = (
            jax.nn.sigmoid(mm(lo, lo + BRANCH_W)).astype(gate_ref.dtype))


def _win(x, shift, scale, gain, w, cos, sin):
    b, l, d = x.shape
    tm = min(512, l)
    n = w.shape[1]
    widths = [(512, F32), (512, ACT_DT), (KV_W, ACT_DT), (KV_W, ACT_DT), (512, ACT_DT),
              (512, F32), (512, F32), (512, ACT_DT), (512, ACT_DT), (3072, ACT_DT)]
    row = lambda bi, i: (bi, i, 0)
    vec = lambda bi, i: (bi, 0, 0)
    return pl.pallas_call(
        _win_kernel,
        out_shape=[jax.ShapeDtypeStruct((b, l, wd), dt) for wd, dt in widths],
        grid=(b, l // tm),
        in_specs=[pl.BlockSpec((1, tm, d), row),
                  pl.BlockSpec((1, 1, d), vec),
                  pl.BlockSpec((1, 1, d), vec),
                  pl.BlockSpec((1, d), lambda bi, i: (0, 0)),
                  pl.BlockSpec((d, n), lambda bi, i: (0, 0), pipeline_mode=pl.Buffered(1)),
                  pl.BlockSpec((tm, LANES), lambda bi, i: (i, 0)),
                  pl.BlockSpec((tm, LANES), lambda bi, i: (i, 0))],
        out_specs=[pl.BlockSpec((1, tm, wd), row) for wd, _ in widths],
        compiler_params=_cparams("parallel", "parallel"),
        name="in_proj",
    )(x, shift, scale, gain, w, cos, sin)


def _s5_tables(lam_re, lam_im, log_dt, b_re, b_im, c_re, c_im, d_skip):
    t_ = S5_T
    nq, gpt, sw = S5_NQ, S5_GPT, S5_SW
    hp = lax.Precision.HIGHEST
    lr = lam_re.astype(F32)
    li = lam_im.astype(F32)
    dt = jnp.exp(log_dt.astype(F32))[..., None]
    xr, xi = dt * lr, dt * li
    mag = jnp.exp(xr)
    ar, ai = mag * jnp.cos(xi), mag * jnp.sin(xi)
    den = lr * lr + li * li
    fr = ((ar - 1.0) * lr + ai * li) / den
    fi = (ai * lr - (ar - 1.0) * li) / den
    eye = jnp.eye(gpt, dtype=F32)

    def blockdiag(p):
        a, b = p.shape[1:]
        p4 = p.astype(F32).reshape(nq, gpt, a, b)
        return (p4[:, :, :, None, :] * eye[None, :, None, :, None]).reshape(nq, gpt * a, gpt * b)

    row = lambda a: a.reshape(a.shape[:-2] + (nq, 1, sw))
    col = lambda a: a.reshape(a.shape[:-2] + (nq, sw, 1))
    brt = blockdiag(jnp.swapaxes(b_re, 1, 2))
    bit = blockdiag(jnp.swapaxes(b_im, 1, 2))
    cxr = blockdiag(jnp.swapaxes(c_re, 1, 2))
    cxi = blockdiag(jnp.swapaxes(c_im, 1, 2))
    bbr = row(fr) * brt - row(fi) * bit
    bbi = row(fr) * bit + row(fi) * brt
    kk = jnp.arange(t_ + 1, dtype=F32)[:, None, None, None]
    pr = jnp.exp(kk * xr) * jnp.cos(kk * xi)
    pi = jnp.exp(kk * xr) * jnp.sin(kk * xi)
    wr = row(pr) * bbr - row(pi) * bbi
    wi = row(pr) * bbi + row(pi) * bbr
    kern = (jnp.einsum('kdqim,qmo->kdqio', wr[:t_], cxr, precision=hp)
            - jnp.einsum('kdqim,qmo->kdqio', wi[:t_], cxi, precision=hp))
    s_idx = jnp.arange(t_)[:, None]
    t_idx = jnp.arange(t_)[None, :]
    sel = lambda m: m[..., None, None, None]
    kst = (jnp.where(sel(t_idx >= s_idx), kern[jnp.clip(t_idx - s_idx, 0, t_ - 1), 0], 0.0)
           + jnp.where(sel(s_idx >= t_idx), kern[jnp.clip(s_idx - t_idx, 0, t_ - 1), 1], 0.0)
           + jnp.where(sel(s_idx == t_idx),
                       jnp.eye(LANES, dtype=F32) * d_skip.astype(F32).reshape(nq, 1, LANES), 0.0))
    toe = jnp.transpose(kst, (2, 0, 3, 1, 4)).reshape(nq, t_ * LANES, t_ * LANES)

    q4 = jnp.stack([jnp.stack([wr[:t_, 0][::-1], wi[:t_, 0][::-1]], 0),
                    jnp.stack([wr[:t_, 1], wi[:t_, 1]], 0)], 0)
    qm = jnp.transpose(q4, (3, 2, 4, 0, 1, 5)).reshape(nq, t_ * LANES, 4 * sw)

    psr = col(jnp.stack([pr[1:, 0], pr[1:, 1][::-1]], 0))
    psi = col(jnp.stack([pi[1:, 0], pi[1:, 1][::-1]], 0))
    p4 = jnp.stack([cxr * psr - cxi * psi, -cxr * psi - cxi * psr], 1)
    pm = jnp.transpose(p4, (3, 0, 1, 4, 2, 5)).reshape(nq, 4 * sw, t_ * LANES)

    dsel = jnp.stack([pr[t_], pi[t_]], 1).reshape(2, 2, nq, sw)
    dec = jnp.transpose(dsel, (2, 0, 1, 3)).reshape(nq, 1, 4 * sw)
    return toe.astype(MXU_DT), qm.astype(MXU_DT), pm.astype(MXU_DT), dec


def _s5_kernel(u_ref, toe_ref, qm_ref, pm_ref, dec_ref, h0_ref, y_ref, hend_ref,
               z_ref, s_ref, hin_ref, *, nj):
    t_, sw = S5_T, S5_SW
    for t in range(t_):
        z_ref[:, t * LANES:(t + 1) * LANES] = u_ref[0, pl.ds(t, nj, stride=t_), :].astype(z_ref.dtype)
    s_ref[...] = jnp.dot(z_ref[...], qm_ref[0], preferred_element_type=F32)
    dec = dec_ref[0]
    dfr, dfi, dbr, dbi = (dec[:, i * sw:(i + 1) * sw] for i in range(4))
    h0 = h0_ref[0, 0]

    def step(j, carry):
        fr, fi, br, bi = carry
        jb = nj - 1 - j
        hin_ref[pl.ds(j, 1), 0 * sw:1 * sw] = fr
        hin_ref[pl.ds(j, 1), 1 * sw:2 * sw] = fi
        hin_ref[pl.ds(jb, 1), 2 * sw:3 * sw] = br
        hin_ref[pl.ds(jb, 1), 3 * sw:4 * sw] = bi
        sfr = s_ref[pl.ds(j, 1), 0 * sw:1 * sw]
        sfi = s_ref[pl.ds(j, 1), 1 * sw:2 * sw]
        sbr = s_ref[pl.ds(jb, 1), 2 * sw:3 * sw]
        sbi = s_ref[pl.ds(jb, 1), 3 * sw:4 * sw]
        return (dfr * fr - dfi * fi + sfr, dfr * fi + dfi * fr + sfi,
                dbr * br - dbi * bi + sbr, dbr * bi + dbi * br + sbi)

    fin = lax.fori_loop(0, nj, step, tuple(h0[:, i * sw:(i + 1) * sw] for i in range(4)), unroll=8)
    for i in range(4):
        hend_ref[0, 0, :, i * sw:(i + 1) * sw] = fin[i]
    y = (jnp.dot(z_ref[...], toe_ref[0], preferred_element_type=F32)
         + jnp.dot(hin_ref[...].astype(MXU_DT), pm_ref[0], preferred_element_type=F32))
    for t in range(t_):
        y_ref[0, pl.ds(t, nj, stride=t_), :] = y[:, t * LANES:(t + 1) * LANES]


def _s5(u, tables, li, h0):
    toe, qm, pm, dec = tables
    b, l, _ = u.shape
    nj = l // S5_T
    kw = S5_T * LANES
    sw4 = 4 * S5_SW
    wmap = lambda q, bi: (li, q, 0, 0)
    return pl.pallas_call(
        functools.partial(_s5_kernel, nj=nj),
        out_shape=[jax.ShapeDtypeStruct((b, l, BRANCH_W), F32),
                   jax.ShapeDtypeStruct((b, S5_NQ, 1, sw4), F32)],
        grid=(S5_NQ, b),
        in_specs=[pl.BlockSpec((1, l, LANES), lambda q, bi: (bi, 0, q)),
                  pl.BlockSpec((None, 1, kw, kw), wmap),
                  pl.BlockSpec((None, 1, kw, sw4), wmap),
                  pl.BlockSpec((None, 1, sw4, kw), wmap),
                  pl.BlockSpec((None, 1, 1, sw4), wmap),
                  pl.BlockSpec((1, 1, 1, sw4), lambda q, bi: (bi, q, 0, 0))],
        out_specs=[pl.BlockSpec((1, l, LANES), lambda q, bi: (bi, 0, q)),
                   pl.BlockSpec((1, 1, 1, sw4), lambda q, bi: (bi, q, 0, 0))],
        scratch_shapes=[pltpu.VMEM((nj, kw), MXU_DT),
                        pltpu.VMEM((nj, sw4), F32),
                        pltpu.VMEM((nj, sw4), F32)],
        compiler_params=_cparams("parallel", "parallel"),
        name="s5_mix",
    )(u, toe, qm, pm, dec, h0)


def _softmax_pv(s, sink_col, v):
    m = jnp.maximum(jnp.max(s, axis=-1, keepdims=True), sink_col)
    p = jnp.exp(s - m)
    den = jnp.sum(p, axis=-1, keepdims=True) + jnp.exp(sink_col - m)
    o = jnp.dot(p.astype(MXU_DT), v, preferred_element_type=F32)
    return o / den


def _attn_group(sink_ref, q_ref, hk, kcat, vcat, bias, rows):
    half = ATT_GRP // 2
    lane = lax.broadcasted_iota(jnp.int32, (1, LANES), 1)
    lo = lane < HEAD_DIM
    k_lo = kcat * lo.astype(kcat.dtype)
    k_hi = kcat * (~lo).astype(kcat.dtype)
    qs = jnp.concatenate([q_ref[0, :, (hk * half + a) * LANES:(hk * half + a + 1) * LANES] for a in range(half)],
                         axis=0)
    dn = (((1,), (1,)), ((), ()))
    s = jnp.concatenate([lax.dot_general(qs, k_lo, dn, preferred_element_type=F32),
                         lax.dot_general(qs, k_hi, dn, preferred_element_type=F32)], axis=0)
    heads = [hk * ATT_GRP + 2 * a for a in range(half)] + [hk * ATT_GRP + 2 * a + 1 for a in range(half)]
    sink_col = jnp.concatenate([jnp.full((rows, 1), sink_ref[h], F32) for h in heads], axis=0)
    if bias is not None:
        s = s + jnp.concatenate([bias] * ATT_GRP, axis=0)
    o = _softmax_pv(s, sink_col, vcat)
    return [jnp.where(lo, o[a * rows:(a + 1) * rows], o[(half + a) * rows:(half + a + 1) * rows])
            for a in range(half)]


def _attn_kernel(sink_ref, q_ref, kp_ref, kc_ref, kn_ref, vp_ref, vc_ref, vn_ref, kx_ref, vx_ref,
                 o_ref, *, nb, lc):
    i = pl.program_id(1)
    blk = ATT_BLOCK
    neg = jnp.float32(-jnp.inf)
    qi = lax.broadcasted_iota(jnp.int32, (blk, blk), 0)
    kj = lax.broadcasted_iota(jnp.int32, (blk, blk), 1)
    bias_p = jnp.where(i > 0, jnp.where(kj >= qi, 0.0, neg), neg)
    bias_n = jnp.where(i < nb - 1, jnp.where(kj <= qi, 0.0, neg), neg)
    bias = jnp.concatenate([jnp.zeros((blk, lc), F32), bias_p, jnp.zeros((blk, blk), F32), bias_n], axis=1)
    half = ATT_GRP // 2
    for hk in range(ATT_KV_HEADS):
        ks = slice(hk * LANES, (hk + 1) * LANES)
        kcat = jnp.concatenate([kx_ref[0, :, ks], kp_ref[0, :, ks], kc_ref[0, :, ks], kn_ref[0, :, ks]], axis=0)
        vcat = jnp.concatenate([vx_ref[0, :, ks], vp_ref[0, :, ks], vc_ref[0, :, ks], vn_ref[0, :, ks]], axis=0)
        for a, slab in enumerate(_attn_group(sink_ref, q_ref, hk, kcat, vcat, bias, blk)):
            c0 = (hk * half + a) * LANES
            o_ref[0, :, c0:c0 + LANES] = slab.astype(o_ref.dtype)


def _attn(q, k, v, kx, vx, sink):
    b, l, _ = q.shape
    lc = kx.shape[1]
    nb = l // ATT_BLOCK
    kvw = KV_W
    cur = lambda bi, i: (bi, i, 0)
    prev = lambda bi, i: (bi, jnp.maximum(i - 1, 0), 0)
    nxt = lambda bi, i: (bi, jnp.minimum(i + 1, nb - 1), 0)
    ctx = lambda bi, i: (bi, 0, 0)
    kvspec = lambda m: pl.BlockSpec((1, ATT_BLOCK, kvw), m)
    return pl.pallas_call(
        functools.partial(_attn_kernel, nb=nb, lc=lc),
        out_shape=jax.ShapeDtypeStruct((b, l, BRANCH_W), ACT_DT),
        grid=(b, nb),
        in_specs=[pl.BlockSpec(memory_space=pltpu.SMEM),
                  pl.BlockSpec((1, ATT_BLOCK, BRANCH_W), cur),
                  kvspec(prev), kvspec(cur), kvspec(nxt),
                  kvspec(prev), kvspec(cur), kvspec(nxt),
                  pl.BlockSpec((1, lc, kvw), ctx),
                  pl.BlockSpec((1, lc, kvw), ctx)],
        out_specs=pl.BlockSpec((1, ATT_BLOCK, BRANCH_W), cur),
        compiler_params=_cparams("parallel", "parallel"),
        name="window_attn",
    )(sink, q, k, k, k, v, v, v, kx, vx)


def _attn_ctx_kernel(sink_ref, q_ref, k_ref, v_ref, o_ref, *, lc):
    half = ATT_GRP // 2
    for hk in range(ATT_KV_HEADS):
        ks = slice(hk * LANES, (hk + 1) * LANES)
        for a, slab in enumerate(_attn_group(sink_ref, q_ref, hk, k_ref[0, :, ks], v_ref[0, :, ks], None, lc)):
            c0 = (hk * half + a) * LANES
            o_ref[0, :, c0:c0 + LANES] = slab.astype(o_ref.dtype)


def _attn_ctx(q, k, v, sink):
    b, lc, _ = q.shape
    kvw = KV_W
    full = lambda bi: (bi, 0, 0)
    return pl.pallas_call(
        functools.partial(_attn_ctx_kernel, lc=lc),
        out_shape=jax.ShapeDtypeStruct((b, lc, BRANCH_W), ACT_DT),
        grid=(b,),
        in_specs=[pl.BlockSpec(memory_space=pltpu.SMEM),
                  pl.BlockSpec((1, lc, BRANCH_W), full),
                  pl.BlockSpec((1, lc, kvw), full),
                  pl.BlockSpec((1, lc, kvw), full)],
        out_specs=pl.BlockSpec((1, lc, BRANCH_W), full),
        compiler_params=_cparams("parallel"),
        name="ctx_attn",
    )(sink, q, k, v)


def _split3(x):
    a = x.astype(jnp.bfloat16)
    r = x - a.astype(F32)
    b = r.astype(jnp.bfloat16)
    c = (r - b.astype(F32)).astype(jnp.bfloat16)
    return a, b, c


def _hg_block(q, z, v, lb, tri, diag, st_ref, d, reverse, cpb):
    c = HG_CHUNK
    w = q.shape[-1]
    s = jax.nn.sigmoid(z)
    f = lb + (1.0 - lb) * s
    logf = jnp.log(f)
    kf = (1.0 - lb) * (1.0 - s)
    cum = sum(jnp.dot(tri, part, preferred_element_type=F32) for part in _split3(logf))
    end_off = 0 if reverse else c - 1
    mid_off = HG_MID if reverse else HG_MID - 1
    cum_end = [cum[ci * c + end_off:ci * c + end_off + 1] for ci in range(cpb)]
    cum_mid = [cum[ci * c + mid_off:ci * c + mid_off + 1] for ci in range(cpb)]
    rows = lambda parts: jnp.concatenate([jnp.broadcast_to(p, (c, w)) for p in parts], axis=0)
    rel = cum - rows(cum_mid)
    qf = q.astype(F32) * jnp.exp(rel)
    kk = kf * jnp.exp(-rel)
    q_in = qf.astype(MXU_DT)
    k_in = kk.astype(MXU_DT)
    q_st = (qf * rows([jnp.exp(m) for m in cum_mid])).astype(MXU_DT)
    k_st = (kk * rows([jnp.exp(e - m) for e, m in zip(cum_end, cum_mid)])).astype(MXU_DT)
    dec = [jnp.exp(e) for e in cum_end]
    vb = v.astype(MXU_DT)
    order = range(cpb - 1, -1, -1) if reverse else range(cpb)
    outs = []
    for h in range(HG_HEADS):
        hs = slice(h * HG_D, (h + 1) * HG_D)
        att = lax.dot_general(q_in[:, hs], k_in[:, hs], (((1,), (1,)), ((), ())), preferred_element_type=F32)
        att = jnp.where(tri > 0, att, 0.0).astype(MXU_DT)
        o = jnp.dot(att, vb[:, hs], preferred_element_type=F32)
        k_bd = jnp.concatenate([k_st[:, hs]] * cpb, axis=1) * diag
        upd = lax.dot_general(vb[:, hs], k_bd, (((0,), (0,)), ((), ())), preferred_element_type=F32)
        st = st_ref[d, h]
        entering = [None] * cpb
        for ci in order:
            entering[ci] = st
            st = st * dec[ci][:, hs] + upd[:, ci * HG_D:(ci + 1) * HG_D]
        st_ref[d, h] = st
        s_cat = jnp.concatenate(entering, axis=1).astype(MXU_DT)
        q_bd = jnp.concatenate([q_st[:, hs]] * cpb, axis=1) * diag
        o = o + lax.dot_general(q_bd, s_cat, (((1,), (1,)), ((), ())), preferred_element_type=F32)
        outs.append(o)
    return jnp.concatenate(outs, axis=1)


def _hgrn_kernel(qf_ref, qb_ref, vf_ref, vb_ref, ff_ref, fb_ref, lb_ref, s0_ref,
                 of_ref, ob_ref, send_ref, st_ref, *, nblk, cpb):
    j = pl.program_id(1)
    c = HG_CHUNK
    tb = cpb * c

    @pl.when(j == 0)
    def _():
        st_ref[...] = s0_ref[0]

    r = lax.broadcasted_iota(jnp.int32, (tb, tb), 0)
    col = lax.broadcasted_iota(jnp.int32, (tb, tb), 1)
    same = (r // c) == (col // c)
    tri_f = (same & (r >= col)).astype(MXU_DT)
    tri_b = (same & (r <= col)).astype(MXU_DT)
    dr = lax.broadcasted_iota(jnp.int32, (tb, cpb * HG_D), 0)
    dc = lax.broadcasted_iota(jnp.int32, (tb, cpb * HG_D), 1)
    diag = ((dr // c) == (dc // HG_D)).astype(MXU_DT)
    of_ref[0] = _hg_block(qf_ref[0], ff_ref[0], vf_ref[0], lb_ref[0:1], tri_f, diag, st_ref, 0,
                          False, cpb).astype(of_ref.dtype)
    ob_ref[0] = _hg_block(qb_ref[0], fb_ref[0], vb_ref[0], lb_ref[1:2], tri_b, diag, st_ref, 1,
                          True, cpb).astype(ob_ref.dtype)

    @pl.when(j == nblk - 1)
    def _():
        send_ref[0] = st_ref[...]


def _hgrn(hq, hi, ff, fb, lb, s0):
    b, l, w = hq.shape
    tb = min(256, l)
    nblk = l // tb
    fw = lambda bi, j: (bi, j, 0)
    bw = lambda bi, j: (bi, nblk - 1 - j, 0)
    st = lambda bi, j: (bi, 0, 0, 0, 0)
    blk = lambda m: pl.BlockSpec((1, tb, w), m)
    return pl.pallas_call(
        functools.partial(_hgrn_kernel, nblk=nblk, cpb=tb // HG_CHUNK),
        out_shape=[jax.ShapeDtypeStruct((b, l, w), ACT_DT),
                   jax.ShapeDtypeStruct((b, l, w), ACT_DT),
                   jax.ShapeDtypeStruct(s0.shape, F32)],
        grid=(b, nblk),
        in_specs=[blk(fw), blk(bw), blk(fw), blk(bw), blk(fw), blk(bw),
                  pl.BlockSpec((2, w), lambda bi, j: (0, 0)),
                  pl.BlockSpec((1,) + s0.shape[1:], st)],
        out_specs=[blk(fw), blk(bw), pl.BlockSpec((1,) + s0.shape[1:], st)],
        scratch_shapes=[pltpu.VMEM(s0.shape[1:], F32)],
        compiler_params=_cparams("parallel", "arbitrary"),
        name="hgrn2_scan",
    )(hq, hq, hi, hi, ff, fb, lb, s0)


def _merge_kernel(x_ref, g1_ref, y5_ref, att_ref, of_ref, ob_ref, hg_ref, gate_ref,
                  wglu_ref, bglu_ref, hgn_ref, wbr_ref, wout_ref, gpost_ref, o_ref):
    d = x_ref.shape[-1]
    ge = jax.nn.gelu(y5_ref[0])
    ya = ge * jax.nn.sigmoid(jnp.dot(ge.astype(MXU_DT), wglu_ref[...], preferred_element_type=F32)
                             + bglu_ref[...])
    o = of_ref[0].astype(F32) + ob_ref[0].astype(F32)
    o = jnp.concatenate([_rms(o[:, h * HG_D:(h + 1) * HG_D]) for h in range(HG_HEADS)], axis=1)
    yc = o * hgn_ref[...] * hg_ref[0].astype(F32)
    ys = (ya.astype(MXU_DT), att_ref[0].astype(MXU_DT), yc.astype(MXU_DT))
    m = None
    for n in range(3):
        zn = jnp.dot(ys[n], wbr_ref[n], preferred_element_type=F32)
        term = gate_ref[0, :, n * d:(n + 1) * d].astype(F32) * zn
        m = term if m is None else m + term
    out = jnp.dot(m.astype(MXU_DT), wout_ref[...], preferred_element_type=F32)
    o_ref[0] = x_ref[0] + g1_ref[0] * (_rms(out) * gpost_ref[...])


def _merge(x, g1, y5, att, o_f, o_b, hg, gate, w_glu, b_glu, hg_norm, w_branch, w_out, g_post):
    b, l, d = x.shape
    tm = min(512, l)
    w = BRANCH_W
    row = lambda bi, i: (bi, i, 0)
    vec = lambda bi, i: (bi, 0, 0)
    c2 = lambda bi, i: (0, 0)
    return pl.pallas_call(
        _merge_kernel,
        out_shape=jax.ShapeDtypeStruct((b, l, d), F32),
        grid=(b, l // tm),
        in_specs=[pl.BlockSpec((1, tm, d), row),
                  pl.BlockSpec((1, 1, d), vec),
                  pl.BlockSpec((1, tm, w), row),
                  pl.BlockSpec((1, tm, w), row),
                  pl.BlockSpec((1, tm, w), row),
                  pl.BlockSpec((1, tm, w), row),
                  pl.BlockSpec((1, tm, w), row),
                  pl.BlockSpec((1, tm, 3 * d), row),
                  pl.BlockSpec((w, w), c2),
                  pl.BlockSpec((1, w), c2),
                  pl.BlockSpec((1, w), c2),
                  pl.BlockSpec((3, w, d), lambda bi, i: (0, 0, 0)),
                  pl.BlockSpec((d, d), c2),
                  pl.BlockSpec((1, d), c2)],
        out_specs=pl.BlockSpec((1, tm, d), row),
        compiler_params=_cparams("parallel", "parallel"),
        name="merge_out",
    )(x, g1, y5, att, o_f, o_b, hg, gate, w_glu, b_glu, hg_norm, w_branch, w_out, g_post)


def _ffn_kernel(x_ref, xp_ref, xn_ref, sh_ref, sc_ref, g2_ref, gpre_ref, gpost_ref,
                wup_ref, cw_ref, cb_ref, wd_ref, o_ref, h_ref, act_ref, *, nrow, tm, tn):
    i = pl.program_id(1)
    halo = SUBLANES
    f = wd_ref.shape[0]
    rows = tm + 2 * halo

    def prep(xx):
        hh = _rms(xx) * gpre_ref[...]
        return hh * (1.0 + sc_ref[0]) + sh_ref[0]

    h_ref[halo:halo + tm] = prep(x_ref[0]).astype(h_ref.dtype)
    h_ref[0:halo] = jnp.where(i > 0, prep(xp_ref[0]), 0.0).astype(h_ref.dtype)
    h_ref[halo + tm:rows] = jnp.where(i < nrow - 1, prep(xn_ref[0]), 0.0).astype(h_ref.dtype)

    def conv(c0):
        p = jnp.dot(h_ref[...], wup_ref[:, c0:c0 + tn], preferred_element_type=F32)
        up = pltpu.roll(p, 1, 0)[halo:halo + tm]
        dn = pltpu.roll(p, rows - 1, 0)[halo:halo + tm]
        cw = cw_ref[:, c0:c0 + tn]
        return cw[0:1] * up + cw[1:2] * p[halo:halo + tm] + cw[2:3] * dn + cb_ref[:, c0:c0 + tn]

    for jt in range(f // tn):
        a = conv(jt * tn)
        g = conv(f + jt * tn)
        act_ref[:, jt * tn:(jt + 1) * tn] = (a * jax.nn.sigmoid(a) * g).astype(act_ref.dtype)
    out = jnp.dot(act_ref[...], wd_ref[...], preferred_element_type=F32)
    o_ref[0] = x_ref[0] + g2_ref[0] * (_rms(out) * gpost_ref[...])


def _ffn(x, shift, scale, g2, g_pre, g_post, w_up, conv_w, conv_b, w_down):
    b, l, d = x.shape
    f = w_down.shape[0]
    tm = min(512, l)
    tn = 256
    nrow = l // tm
    hb = tm // SUBLANES
    row = lambda bi, i: (bi, i, 0)
    vec = lambda bi, i: (bi, 0, 0)
    c2 = lambda bi, i: (0, 0)
    resident = lambda shape: pl.BlockSpec(shape, c2, pipeline_mode=pl.Buffered(1))
    return pl.pallas_call(
        functools.partial(_ffn_kernel, nrow=nrow, tm=tm, tn=tn),
        out_shape=jax.ShapeDtypeStruct((b, l, d), F32),
        grid=(b, nrow),
        in_specs=[pl.BlockSpec((1, tm, d), row),
                  pl.BlockSpec((1, SUBLANES, d), lambda bi, i: (bi, jnp.maximum(i * hb - 1, 0), 0)),
                  pl.BlockSpec((1, SUBLANES, d), lambda bi, i: (bi, jnp.minimum((i + 1) * hb, l // SUBLANES - 1), 0)),
                  pl.BlockSpec((1, 1, d), vec),
                  pl.BlockSpec((1, 1, d), vec),
                  pl.BlockSpec((1, 1, d), vec),
                  pl.BlockSpec((1, d), c2),
                  pl.BlockSpec((1, d), c2),
                  resident((d, 2 * f)),
                  resident((CONV_W, 2 * f)),
                  resident((1, 2 * f)),
                  resident((f, d))],
        out_specs=pl.BlockSpec((1, tm, d), row),
        scratch_shapes=[pltpu.VMEM((tm + 2 * SUBLANES, d), MXU_DT),
                        pltpu.VMEM((tm, f), MXU_DT)],
        compiler_params=_cparams("parallel", "parallel"),
        name="conv_ffn",
    )(x, x, x, shift, scale, g2, g_pre, g_post, w_up, conv_w, conv_b, w_down)


def _rope_tables(l):
    rows = l // GRID_W
    row = jnp.repeat(jnp.arange(rows, dtype=F32), GRID_W)
    col = jnp.tile(jnp.arange(GRID_W, dtype=F32), rows)
    nf = HEAD_DIM // 4
    inv = ROPE_BASE ** (-jnp.arange(nf, dtype=F32) / nf)
    ang = jnp.concatenate([row[:, None] * inv, col[:, None] * inv], axis=-1)
    cos, sin = jnp.cos(ang), jnp.sin(ang)
    cos = jnp.tile(jnp.concatenate([cos, cos], axis=-1), (1, LANES // HEAD_DIM))
    sin = jnp.tile(jnp.concatenate([-sin, sin], axis=-1), (1, LANES // HEAD_DIM))
    return cos, sin


def _dup_kv_columns(w):
    hd = HEAD_DIM
    kv0 = 2 * BRANCH_W
    heads = [w[:, kv0 + i * hd:kv0 + (i + 1) * hd] for i in range(2 * ATT_KV_HEADS)]
    dup = [h for h in heads for _ in range(2)]
    return jnp.concatenate([w[:, :kv0]] + dup + [w[:, kv0 + 2 * ATT_KV_HEADS * hd:]], axis=1)


def _lower_bounds(logits):
    pr = jax.nn.softmax(logits.astype(F32), axis=0)
    cs = jnp.cumsum(pr, axis=0)
    return cs - cs[:1]


def kernel(x, c, ctx, c_ctx, w_mod, b_mod, norm_g, w_in, s5_lam_re, s5_lam_im, s5_log_dt, s5_b_re, s5_b_im, s5_c_re, s5_c_im, s5_d, s5_w_glu, s5_b_glu, att_sink, hg_lb_logits, hg_norm_g, w_branch, w_out, ffn_w_up, ffn_conv_w, ffn_conv_b, ffn_w_down):
    b, l, d = x.shape
    lc = ctx.shape[1]
    depth = w_in.shape[0]
    nc = c.shape[0]
    nrows = -(-(nc + 1) // SUBLANES) * SUBLANES
    cond = jnp.zeros((nrows, d), F32).at[:nc].set(c).at[nc].set(c_ctx)
    mods = _mod_all(cond, w_mod, b_mod)
    cos_l, sin_l = _rope_tables(l)
    cos_c = jnp.ones((lc, LANES), F32)
    sin_c = jnp.zeros((lc, LANES), F32)
    lb_all = _lower_bounds(hg_lb_logits)
    tables_all = jax.vmap(_s5_tables)(s5_lam_re, s5_lam_im, s5_log_dt, s5_b_re, s5_b_im,
                                      s5_c_re, s5_c_im, s5_d)

    xl, xc = x, ctx
    for li in range(depth):
        with_ctx_out = li < depth - 1
        ml = mods[li, :nc].reshape(nc, 1, 6 * d)
        mc = jnp.broadcast_to(mods[li, nc].reshape(1, 1, 6 * d), (b, 1, 6 * d))
        sl = lambda m, k: m[:, :, k * d:(k + 1) * d]
        gains = norm_g[li].astype(F32)
        w_in_b = _dup_kv_columns(w_in[li]).astype(MXU_DT)
        sink = att_sink[li].astype(F32)
        w_glu = s5_w_glu[li].astype(MXU_DT)
        b_glu = s5_b_glu[li].astype(F32).reshape(1, BRANCH_W)
        hgn = hg_norm_g[li].astype(F32).reshape(1, BRANCH_W)
        wbr = w_branch[li].astype(MXU_DT)
        wout = w_out[li].astype(MXU_DT)
        wup = ffn_w_up[li].astype(MXU_DT)
        wdn = ffn_w_down[li].astype(MXU_DT)
        cw = ffn_conv_w[li].astype(F32)
        cb = ffn_conv_b[li].astype(F32).reshape(1, -1)
        lb = lb_all[li]

        pc = _win(xc, sl(mc, 0), sl(mc, 1), gains[0:1], w_in_b, cos_c, sin_c)
        u_c, q_c, k_c, v_c, hq_c, ff_c, fb_c, hi_c, hg_c, gate_c = pc
        zero_h = jnp.zeros((b, S5_NQ, 1, 4 * S5_SW), F32)
        y5_c, h_ctx = _s5(u_c, tables_all, li, zero_h)
        zero_s = jnp.zeros((b, 2, HG_HEADS, HG_D, HG_D), F32)
        of_c, ob_c, s_ctx = _hgrn(hq_c, hi_c, ff_c, fb_c, lb, zero_s)

        pl_ = _win(xl, sl(ml, 0), sl(ml, 1), gains[0:1], w_in_b, cos_l, sin_l)
        u_l, q_l, k_l, v_l, hq_l, ff_l, fb_l, hi_l, hg_l, gate_l = pl_
        y5_l, _ = _s5(u_l, tables_all, li, h_ctx)
        att_l = _attn(q_l, k_l, v_l, k_c, v_c, sink)
        of_l, ob_l, _ = _hgrn(hq_l, hi_l, ff_l, fb_l, lb, s_ctx)
        xl = _merge(xl, sl(ml, 2), y5_l, att_l, of_l, ob_l, hg_l, gate_l,
                    w_glu, b_glu, hgn, wbr, wout, gains[1:2])
        xl = _ffn(xl, sl(ml, 3), sl(ml, 4), sl(ml, 5), gains[2:3], gains[3:4], wup, cw, cb, wdn)

        if with_ctx_out:
            att_c = _attn_ctx(q_c, k_c, v_c, sink)
            xc = _merge(xc, sl(mc, 2), y5_c, att_c, of_c, ob_c, hg_c, gate_c,
                        w_glu, b_glu, hgn, wbr, wout, gains[1:2])
            xc = _ffn(xc, sl(mc, 3), sl(mc, 4), sl(mc, 5), gains[2:3], gains[3:4], wup, cw, cb, wdn)
    return xl
```

```python
import functools
import math

import jax
import jax.numpy as jnp
from jax import lax
from jax.experimental import pallas as pl
from jax.experimental.pallas import tpu as pltpu

F32 = jnp.float32
MXU_DT = jnp.bfloat16
ACT_DT = jnp.bfloat16
EPS = 1e-6

BRANCH_W = 512
S5_GROUP = 16
S5_GROUPS = BRANCH_W // S5_GROUP
S5_STATE = 64
HEAD_DIM = 64
ATT_HEADS = 8
ATT_KV_HEADS = 2
ATT_GRP = ATT_HEADS // ATT_KV_HEADS
ATT_BLOCK = 128
GRID_W = 64
ROPE_BASE = 10000.0
HG_HEADS = 4
HG_D = 128
CONV_W = 3

LANES = 128
SUBLANES = 8
VMEM_LIMIT = 56 * 1024 * 1024

S5_T = 8
S5_GPT = LANES // S5_GROUP
S5_NQ = BRANCH_W // LANES
S5_SW = S5_GPT * S5_STATE
HG_CHUNK = 64
LOG2E = math.log2(math.e)
HG_MID = HG_CHUNK // 2


def _cparams(*sem):
    return pltpu.CompilerParams(dimension_semantics=sem, vmem_limit_bytes=VMEM_LIMIT)


def _rms(x):
    return x * lax.rsqrt(jnp.mean(x * x, axis=-1, keepdims=True) + EPS)


def _mod_kernel(a_ref, w_ref, b_ref, o_ref):
    a = a_ref[...]
    a = a * jax.nn.sigmoid(a)
    o_ref[0] = lax.dot_general(a, w_ref[0], (((1,), (0,)), ((), ())),
                               precision=lax.Precision.HIGHEST,
                               preferred_element_type=F32) + b_ref[0]


def _mod_all(cond, w_mod, b_mod):
    depth, d, n = w_mod.shape
    r = cond.shape[0]
    tn = 1536
    return pl.pallas_call(
        _mod_kernel,
        out_shape=jax.ShapeDtypeStruct((depth, r, n), F32),
        grid=(depth, n // tn),
        in_specs=[pl.BlockSpec((r, d), lambda l, j: (0, 0)),
                  pl.BlockSpec((1, d, tn), lambda l, j: (l, 0, j)),
                  pl.BlockSpec((1, 1, tn), lambda l, j: (l, 0, j))],
        out_specs=pl.BlockSpec((1, r, tn), lambda l, j: (l, 0, j)),
        compiler_params=_cparams("parallel", "parallel"),
        name="adaln_mod",
    )(cond, w_mod, b_mod.reshape(depth, 1, n))


_C_U = (0, 512)
_C_Q = (512, 1024)
_C_K = (1024, 1280)
_C_V = (1280, 1536)
_C_HQ = (1536, 2048)
_C_FF = (2048, 2560)
_C_FB = (2560, 3072)
_C_HI = (3072, 3584)
_C_HG = (3584, 4096)
_C_GATE = (4096, 7168)
KV_W = 2 * ATT_KV_HEADS * HEAD_DIM


def _rope(z, cos, sin):
    lane = lax.broadcasted_iota(jnp.int32, z.shape, 1)
    first = (lane & (HEAD_DIM // 2)) == 0
    partner = jnp.where(first, pltpu.roll(z, LANES - HEAD_DIM // 2, 1), pltpu.roll(z, HEAD_DIM // 2, 1))
    return z * cos + partner * sin


def _win_kernel(x_ref, sh_ref, sc_ref, g_ref, w_ref, cos_ref, sin_ref,
                u_ref, q_ref, k_ref, v_ref, hq_ref, ff_ref, fb_ref, hi_ref, hg_ref, gate_ref):
    x = x_ref[0]
    h = _rms(x) * g_ref[...]
    h = h * (1.0 + sc_ref[0]) + sh_ref[0]
    hb = h.astype(MXU_DT)

    def mm(lo, hi):
        return jnp.dot(hb, w_ref[:, lo:hi], preferred_element_type=F32)

    u_ref[0] = mm(*_C_U)
    cos = cos_ref[...]
    sin = sin_ref[...]
    zq = mm(*_C_Q)
    for s in range(BRANCH_W // LANES):
        z = _rope(zq[:, s * LANES:(s + 1) * LANES], cos, sin) * (HEAD_DIM ** -0.5 * LOG2E)
        q_ref[0, :, s * LANES:(s + 1) * LANES] = z.astype(q_ref.dtype)
    zkv = mm(_C_K[0], _C_V[1])
    for s in range(KV_W // LANES):
        k_ref[0, :, s * LANES:(s + 1) * LANES] = _rope(zkv[:, s * LANES:(s + 1) * LANES], cos, sin).astype(k_ref.dtype)
    v_ref[0] = zkv[:, KV_W:].astype(v_ref.dtype)
    z = mm(*_C_HQ)
    hq_ref[0] = (z * jax.nn.sigmoid(z)).astype(hq_ref.dtype)
    ff_ref[0] = mm(*_C_FF)
    fb_ref[0] = mm(*_C_FB)
    hi_ref[0] = mm(*_C_HI).astype(hi_ref.dtype)
    hg_ref[0] = jax.nn.sigmoid(mm(*_C_HG)).astype(hg_ref.dtype)
    for s in range((_C_GATE[1] - _C_GATE[0]) // BRANCH_W):
        lo = _C_GATE[0] + s * BRANCH_W
        gate_ref[0, :, s * BRANCH_W:(s + 1) * BRANCH_W] = (
            jax.nn.sigmoid(mm(lo, lo + BRANCH_W)).astype(gate_ref.dtype))


def _win(x, shift, scale, gain, w, li, cos, sin):
    b, l, d = x.shape
    tm = min(512, l)
    n = w.shape[-1]
    widths = [(512, F32), (512, ACT_DT), (KV_W, ACT_DT), (KV_W, ACT_DT), (512, ACT_DT),
              (512, F32), (512, F32), (512, ACT_DT), (512, ACT_DT), (3072, ACT_DT)]
    row = lambda bi, i: (bi, i, 0)
    vec = lambda bi, i: (bi, 0, 0)
    return pl.pallas_call(
        _win_kernel,
        out_shape=[jax.ShapeDtypeStruct((b, l, wd), dt) for wd, dt in widths],
        grid=(b, l // tm),
        in_specs=[pl.BlockSpec((1, tm, d), row),
                  pl.BlockSpec((1, 1, d), vec),
                  pl.BlockSpec((1, 1, d), vec),
                  pl.BlockSpec((1, d), lambda bi, i: (0, 0)),
                  pl.BlockSpec((None, d, n), lambda bi, i: (li, 0, 0), pipeline_mode=pl.Buffered(1)),
                  pl.BlockSpec((tm, LANES), lambda bi, i: (i, 0)),
                  pl.BlockSpec((tm, LANES), lambda bi, i: (i, 0))],
        out_specs=[pl.BlockSpec((1, tm, wd), row) for wd, _ in widths],
        compiler_params=_cparams("parallel", "parallel"),
        name="in_proj",
    )(x, shift, scale, gain, w, cos, sin)


def _s5_tables(lam_re, lam_im, log_dt, b_re, b_im, c_re, c_im, d_skip):
    t_, g_, n_, c_ = S5_T, S5_GROUPS, S5_STATE, S5_GROUP
    nq, gpt, sw = S5_NQ, S5_GPT, S5_SW
    hp = lax.Precision.HIGHEST
    lr = lam_re.astype(F32)
    li = lam_im.astype(F32)
    dt = jnp.exp(log_dt.astype(F32))[..., None]
    xr, xi = dt * lr, dt * li
    mag = jnp.exp(xr)
    ar, ai = mag * jnp.cos(xi), mag * jnp.sin(xi)
    den = lr * lr + li * li
    fr = ((ar - 1.0) * lr + ai * li) / den
    fi = (ai * lr - (ar - 1.0) * li) / den
    br, bi = b_re.astype(F32), b_im.astype(F32)
    bbr = fr[..., None] * br - fi[..., None] * bi
    bbi = fr[..., None] * bi + fi[..., None] * br
    kk = jnp.arange(t_ + 1, dtype=F32)[:, None, None, None]
    pr = jnp.exp(kk * xr) * jnp.cos(kk * xi)
    pi = jnp.exp(kk * xr) * jnp.sin(kk * xi)
    wr = pr[..., None] * bbr - pi[..., None] * bbi
    wi = pr[..., None] * bbi + pi[..., None] * bbr
    cr, ci = c_re.astype(F32), c_im.astype(F32)
    kern = (jnp.einsum('gon,kdgni->kdgoi', cr, wr[:t_], precision=hp)
            - jnp.einsum('gon,kdgni->kdgoi', ci, wi[:t_], precision=hp))
    s_idx = jnp.arange(t_)[:, None]
    t_idx = jnp.arange(t_)[None, :]
    sel = lambda m: m[..., None, None, None]
    skip = jnp.eye(c_, dtype=F32) * d_skip.astype(F32).reshape(g_, 1, c_)
    kst = (jnp.where(sel(t_idx >= s_idx), kern[jnp.clip(t_idx - s_idx, 0, t_ - 1), 0], 0.0)
           + jnp.where(sel(s_idx >= t_idx), kern[jnp.clip(s_idx - t_idx, 0, t_ - 1), 1], 0.0)
           + jnp.where(sel(s_idx == t_idx), skip, 0.0))
    toe_c = jnp.transpose(kst.reshape(t_, t_, nq, gpt, c_, c_), (2, 0, 3, 5, 1, 4))
    toe_c = toe_c.reshape(nq, t_ * LANES, t_ * c_)
    q6 = jnp.stack([jnp.stack([wr[:t_, 0][::-1], wi[:t_, 0][::-1]], 0),
                    jnp.stack([wr[:t_, 1], wi[:t_, 1]], 0)], 0)
    qm_c = jnp.transpose(q6.reshape(2, 2, t_, nq, gpt, n_, c_), (3, 2, 4, 6, 0, 1, 5))
    qm_c = qm_c.reshape(nq, t_ * LANES, 4 * n_)
    psr = jnp.stack([pr[1:, 0], pr[1:, 1][::-1]], 0)[..., None]
    psi = jnp.stack([pi[1:, 0], pi[1:, 1][::-1]], 0)[..., None]
    crt = jnp.swapaxes(cr, 1, 2)
    cit = jnp.swapaxes(ci, 1, 2)
    p6 = jnp.stack([crt * psr - cit * psi, -crt * psi - cit * psr], 1)
    pm_c = jnp.transpose(p6.reshape(2, 2, t_, nq, gpt, n_, c_), (3, 0, 1, 4, 5, 2, 6))
    pm_c = pm_c.reshape(nq, 4 * sw, t_ * c_)

    def expand(tab, rep_rows, rep_cols, inner):
        rows, cols = tab.shape[1], tab.shape[2]
        src = lax.broadcasted_iota(jnp.int32, (cols, cols * gpt), 0)
        dst = lax.broadcasted_iota(jnp.int32, (cols, cols * gpt), 1)
        rep = ((src // inner == dst // (inner * gpt)) & (src % inner == dst % inner)).astype(MXU_DT)
        out = jnp.einsum('qrc,cd->qrd', tab.astype(MXU_DT), rep, preferred_element_type=F32)
        rg = lax.broadcasted_iota(jnp.int32, (rows, cols * gpt), 0) // rep_rows % gpt
        cg = lax.broadcasted_iota(jnp.int32, (rows, cols * gpt), 1) // rep_cols % gpt
        return jnp.where(rg == cg, out, 0.0).astype(MXU_DT)

    toe = expand(toe_c, c_, c_, c_)
    qm = expand(qm_c, c_, n_, n_)
    pm = expand(pm_c, n_, c_, c_)
    dsel = jnp.stack([pr[t_], pi[t_]], 1).reshape(2, 2, nq, sw)
    dec = jnp.transpose(dsel, (2, 0, 1, 3)).reshape(nq, 1, 4 * sw)
    return toe, qm, pm, dec


def _s5_kernel(u_ref, toe_ref, qm_ref, pm_ref, dec_ref, h0_ref, y_ref, hend_ref,
               z_ref, s_ref, hin_ref, *, nj):
    t_, sw = S5_T, S5_SW
    for t in range(t_):
        z_ref[:, t * LANES:(t + 1) * LANES] = u_ref[0, pl.ds(t, nj, stride=t_), :].astype(z_ref.dtype)
    s_ref[...] = jnp.dot(z_ref[...], qm_ref[0], preferred_element_type=F32)
    dec = dec_ref[0]
    dfr, dfi, dbr, dbi = (dec[:, i * sw:(i + 1) * sw] for i in range(4))
    h0 = h0_ref[0, 0]

    def step(j, carry):
        fr, fi, br, bi = carry
        jb = nj - 1 - j
        hin_ref[pl.ds(j, 1), 0 * sw:1 * sw] = fr
        hin_ref[pl.ds(j, 1), 1 * sw:2 * sw] = fi
        hin_ref[pl.ds(jb, 1), 2 * sw:3 * sw] = br
        hin_ref[pl.ds(jb, 1), 3 * sw:4 * sw] = bi
        sfr = s_ref[pl.ds(j, 1), 0 * sw:1 * sw]
        sfi = s_ref[pl.ds(j, 1), 1 * sw:2 * sw]
        sbr = s_ref[pl.ds(jb, 1), 2 * sw:3 * sw]
        sbi = s_ref[pl.ds(jb, 1), 3 * sw:4 * sw]
        return (dfr * fr - dfi * fi + sfr, dfr * fi + dfi * fr + sfi,
                dbr * br - dbi * bi + sbr, dbr * bi + dbi * br + sbi)

    fin = lax.fori_loop(0, nj, step, tuple(h0[:, i * sw:(i + 1) * sw] for i in range(4)), unroll=8)
    for i in range(4):
        hend_ref[0, 0, :, i * sw:(i + 1) * sw] = fin[i]
    y = (jnp.dot(z_ref[...], toe_ref[0], preferred_element_type=F32)
         + jnp.dot(hin_ref[...].astype(MXU_DT), pm_ref[0], preferred_element_type=F32))
    for t in range(t_):
        y_ref[0, pl.ds(t, nj, stride=t_), :] = y[:, t * LANES:(t + 1) * LANES]


def _s5(u, tables, li, h0):
    toe, qm, pm, dec = tables
    b, l, _ = u.shape
    nj = l // S5_T
    kw = S5_T * LANES
    sw4 = 4 * S5_SW
    wmap = lambda q, bi: (li, q, 0, 0)
    return pl.pallas_call(
        functools.partial(_s5_kernel, nj=nj),
        out_shape=[jax.ShapeDtypeStruct((b, l, BRANCH_W), F32),
                   jax.ShapeDtypeStruct((b, S5_NQ, 1, sw4), F32)],
        grid=(S5_NQ, b),
        in_specs=[pl.BlockSpec((1, l, LANES), lambda q, bi: (bi, 0, q)),
                  pl.BlockSpec((None, 1, kw, kw), wmap),
                  pl.BlockSpec((None, 1, kw, sw4), wmap),
                  pl.BlockSpec((None, 1, sw4, kw), wmap),
                  pl.BlockSpec((None, 1, 1, sw4), wmap),
                  pl.BlockSpec((1, 1, 1, sw4), lambda q, bi: (bi, q, 0, 0))],
        out_specs=[pl.BlockSpec((1, l, LANES), lambda q, bi: (bi, 0, q)),
                   pl.BlockSpec((1, 1, 1, sw4), lambda q, bi: (bi, q, 0, 0))],
        scratch_shapes=[pltpu.VMEM((nj, kw), MXU_DT),
                        pltpu.VMEM((nj, sw4), F32),
                        pltpu.VMEM((nj, sw4), F32)],
        compiler_params=_cparams("parallel", "parallel"),
        name="s5_mix",
    )(u, toe, qm, pm, dec, h0)


def _softmax_pv(s, sink_col, v):
    m = jnp.maximum(jnp.max(s, axis=-1, keepdims=True), sink_col)
    p = jnp.exp2(s - m)
    den = jnp.sum(p, axis=-1, keepdims=True) + jnp.exp2(sink_col - m)
    o = jnp.dot(p.astype(MXU_DT), v, preferred_element_type=F32)
    return o / den


def _attn_group(sink_ref, q_ref, r0, hk, kcat, vcat, bias, rows):
    half = ATT_GRP // 2
    lane = lax.broadcasted_iota(jnp.int32, (1, LANES), 1)
    lo = lane < HEAD_DIM
    k_lo = kcat * lo.astype(kcat.dtype)
    k_hi = kcat * (~lo).astype(kcat.dtype)
    qs = jnp.concatenate([q_ref[0, r0:r0 + rows, (hk * half + a) * LANES:(hk * half + a + 1) * LANES]
                          for a in range(half)], axis=0)
    dn = (((1,), (1,)), ((), ()))
    s = jnp.concatenate([lax.dot_general(qs, k_lo, dn, preferred_element_type=F32),
                         lax.dot_general(qs, k_hi, dn, preferred_element_type=F32)], axis=0)
    heads = [hk * ATT_GRP + 2 * a for a in range(half)] + [hk * ATT_GRP + 2 * a + 1 for a in range(half)]
    sink_col = jnp.concatenate([jnp.full((rows, 1), sink_ref[h], F32) for h in heads], axis=0)
    if bias is not None:
        s = s + jnp.concatenate([bias] * ATT_GRP, axis=0)
    o = _softmax_pv(s, sink_col, vcat)
    return [jnp.where(lo, o[a * rows:(a + 1) * rows], o[(half + a) * rows:(half + a + 1) * rows])
            for a in range(half)]


def _attn_kernel(sink_ref, q_ref, kp_ref, kc_ref, kn_ref, vp_ref, vc_ref, vn_ref, kx_ref, vx_ref,
                 o_ref, *, nsteps, lc, qb):
    i = pl.program_id(1)
    blk = ATT_BLOCK
    neg = jnp.float32(-jnp.inf)
    qi = lax.broadcasted_iota(jnp.int32, (blk, blk), 0)
    kj = lax.broadcasted_iota(jnp.int32, (blk, blk), 1)
    win_p = jnp.where(kj >= qi, 0.0, neg)
    win_n = jnp.where(kj <= qi, 0.0, neg)
    edge_p = jnp.where(i > 0, win_p, neg)
    edge_n = jnp.where(i < nsteps - 1, win_n, neg)
    zc = jnp.zeros((blk, lc), F32)
    zb = jnp.zeros((blk, blk), F32)
    half = ATT_GRP // 2
    for hk in range(ATT_KV_HEADS):
        ks = slice(hk * LANES, (hk + 1) * LANES)
        kblocks = ([kp_ref[0, :, ks]] + [kc_ref[0, sb * blk:(sb + 1) * blk, ks] for sb in range(qb)]
                   + [kn_ref[0, :, ks]])
        vblocks = ([vp_ref[0, :, ks]] + [vc_ref[0, sb * blk:(sb + 1) * blk, ks] for sb in range(qb)]
                   + [vn_ref[0, :, ks]])
        for sb in range(qb):
            kcat = jnp.concatenate([kx_ref[0, :, ks]] + kblocks[sb:sb + 3], axis=0)
            vcat = jnp.concatenate([vx_ref[0, :, ks]] + vblocks[sb:sb + 3], axis=0)
            bias = jnp.concatenate([zc, edge_p if sb == 0 else win_p, zb,
                                    edge_n if sb == qb - 1 else win_n], axis=1)
            for a, slab in enumerate(_attn_group(sink_ref, q_ref, sb * blk, hk, kcat, vcat, bias, blk)):
                c0 = (hk * half + a) * LANES
                o_ref[0, sb * blk:(sb + 1) * blk, c0:c0 + LANES] = slab.astype(o_ref.dtype)


def _attn(q, k, v, kx, vx, sink):
    b, l, _ = q.shape
    lc = kx.shape[1]
    nb = l // ATT_BLOCK
    qb = 2 if nb % 2 == 0 else 1
    nsteps = nb // qb
    cur = lambda bi, i: (bi, i, 0)
    prev = lambda bi, i: (bi, jnp.maximum(qb * i - 1, 0), 0)
    nxt = lambda bi, i: (bi, jnp.minimum(qb * (i + 1), nb - 1), 0)
    ctx = lambda bi, i: (bi, 0, 0)
    edge = lambda m: pl.BlockSpec((1, ATT_BLOCK, KV_W), m)
    body = pl.BlockSpec((1, qb * ATT_BLOCK, KV_W), cur)
    return pl.pallas_call(
        functools.partial(_attn_kernel, nsteps=nsteps, lc=lc, qb=qb),
        out_shape=jax.ShapeDtypeStruct((b, l, BRANCH_W), ACT_DT),
        grid=(b, nsteps),
        in_specs=[pl.BlockSpec(memory_space=pltpu.SMEM),
                  pl.BlockSpec((1, qb * ATT_BLOCK, BRANCH_W), cur),
                  edge(prev), body, edge(nxt),
                  edge(prev), body, edge(nxt),
                  pl.BlockSpec((1, lc, KV_W), ctx),
                  pl.BlockSpec((1, lc, KV_W), ctx)],
        out_specs=pl.BlockSpec((1, qb * ATT_BLOCK, BRANCH_W), cur),
        compiler_params=_cparams("parallel", "parallel"),
        name="window_attn",
    )(sink, q, k, k, k, v, v, v, kx, vx)


def _attn_ctx_kernel(sink_ref, q_ref, k_ref, v_ref, o_ref, *, lc):
    half = ATT_GRP // 2
    for hk in range(ATT_KV_HEADS):
        ks = slice(hk * LANES, (hk + 1) * LANES)
        for a, slab in enumerate(_attn_group(sink_ref, q_ref, 0, hk, k_ref[0, :, ks], v_ref[0, :, ks], None, lc)):
            c0 = (hk * half + a) * LANES
            o_ref[0, :, c0:c0 + LANES] = slab.astype(o_ref.dtype)


def _attn_ctx(q, k, v, sink):
    b, lc, _ = q.shape
    kvw = KV_W
    full = lambda bi: (bi, 0, 0)
    return pl.pallas_call(
        functools.partial(_attn_ctx_kernel, lc=lc),
        out_shape=jax.ShapeDtypeStruct((b, lc, BRANCH_W), ACT_DT),
        grid=(b,),
        in_specs=[pl.BlockSpec(memory_space=pltpu.SMEM),
                  pl.BlockSpec((1, lc, BRANCH_W), full),
                  pl.BlockSpec((1, lc, kvw), full),
                  pl.BlockSpec((1, lc, kvw), full)],
        out_specs=pl.BlockSpec((1, lc, BRANCH_W), full),
        compiler_params=_cparams("parallel"),
        name="ctx_attn",
    )(sink, q, k, v)


def _split2(x):
    a = x.astype(jnp.bfloat16)
    b = (x - a.astype(F32)).astype(jnp.bfloat16)
    return a, b


def _hg_block(q, z, v, lb, tri, diag, st_ref, d, reverse, cpb):
    c = HG_CHUNK
    w = q.shape[-1]
    s = jax.nn.sigmoid(z)
    f = lb + (1.0 - lb) * s
    logf = jnp.log(f)
    kf = (1.0 - lb) * (1.0 - s)
    cum = sum(jnp.dot(tri, part, preferred_element_type=F32) for part in _split2(logf))
    end_off = 0 if reverse else c - 1
    mid_off = HG_MID if reverse else HG_MID - 1
    cum_end = [cum[ci * c + end_off:ci * c + end_off + 1] for ci in range(cpb)]
    cum_mid = [cum[ci * c + mid_off:ci * c + mid_off + 1] for ci in range(cpb)]
    rows = lambda parts: jnp.concatenate([jnp.broadcast_to(p, (c, w)) for p in parts], axis=0)
    rel = cum - rows(cum_mid)
    qf = q.astype(F32) * jnp.exp(rel)
    kk = kf * jnp.exp(-rel)
    q_in = qf.astype(MXU_DT)
    k_in = kk.astype(MXU_DT)
    q_st = (qf * rows([jnp.exp(m) for m in cum_mid])).astype(MXU_DT)
    k_st = (kk * rows([jnp.exp(e - m) for e, m in zip(cum_end, cum_mid)])).astype(MXU_DT)
    dec = [jnp.exp(e) for e in cum_end]
    vb = v.astype(MXU_DT)
    order = range(cpb - 1, -1, -1) if reverse else range(cpb)
    outs = []
    for h in range(HG_HEADS):
        hs = slice(h * HG_D, (h + 1) * HG_D)
        att = lax.dot_general(q_in[:, hs], k_in[:, hs], (((1,), (1,)), ((), ())), preferred_element_type=F32)
        att = jnp.where(tri > 0, att, 0.0).astype(MXU_DT)
        o = jnp.dot(att, vb[:, hs], preferred_element_type=F32)
        k_bd = jnp.concatenate([k_st[:, hs]] * cpb, axis=1) * diag
        upd = lax.dot_general(vb[:, hs], k_bd, (((0,), (0,)), ((), ())), preferred_element_type=F32)
        st = st_ref[d, h]
        entering = [None] * cpb
        for ci in order:
            entering[ci] = st
            st = st * dec[ci][:, hs] + upd[:, ci * HG_D:(ci + 1) * HG_D]
        st_ref[d, h] = st
        s_cat = jnp.concatenate(entering, axis=1).astype(MXU_DT)
        q_bd = jnp.concatenate([q_st[:, hs]] * cpb, axis=1) * diag
        o = o + lax.dot_general(q_bd, s_cat, (((1,), (1,)), ((), ())), preferred_element_type=F32)
        outs.append(o)
    return jnp.concatenate(outs, axis=1)


def _hgrn_kernel(qf_ref, qb_ref, vf_ref, vb_ref, ff_ref, fb_ref, lb_ref, s0_ref,
                 of_ref, ob_ref, send_ref, st_ref, *, nblk, cpb):
    j = pl.program_id(1)
    c = HG_CHUNK
    tb = cpb * c

    @pl.when(j == 0)
    def _():
        st_ref[...] = s0_ref[0]

    r = lax.broadcasted_iota(jnp.int32, (tb, tb), 0)
    col = lax.broadcasted_iota(jnp.int32, (tb, tb), 1)
    same = (r // c) == (col // c)
    tri_f = (same & (r >= col)).astype(MXU_DT)
    tri_b = (same & (r <= col)).astype(MXU_DT)
    dr = lax.broadcasted_iota(jnp.int32, (tb, cpb * HG_D), 0)
    dc = lax.broadcasted_iota(jnp.int32, (tb, cpb * HG_D), 1)
    diag = ((dr // c) == (dc // HG_D)).astype(MXU_DT)
    of_ref[0] = _hg_block(qf_ref[0], ff_ref[0], vf_ref[0], lb_ref[0:1], tri_f, diag, st_ref, 0,
                          False, cpb).astype(of_ref.dtype)
    ob_ref[0] = _hg_block(qb_ref[0], fb_ref[0], vb_ref[0], lb_ref[1:2], tri_b, diag, st_ref, 1,
                          True, cpb).astype(ob_ref.dtype)

    @pl.when(j == nblk - 1)
    def _():
        send_ref[0] = st_ref[...]


def _hgrn(hq, hi, ff, fb, lb, s0):
    b, l, w = hq.shape
    tb = min(256, l)
    nblk = l // tb
    fw = lambda bi, j: (bi, j, 0)
    bw = lambda bi, j: (bi, nblk - 1 - j, 0)
    st = lambda bi, j: (bi, 0, 0, 0, 0)
    blk = lambda m: pl.BlockSpec((1, tb, w), m)
    return pl.pallas_call(
        functools.partial(_hgrn_kernel, nblk=nblk, cpb=tb // HG_CHUNK),
        out_shape=[jax.ShapeDtypeStruct((b, l, w), ACT_DT),
                   jax.ShapeDtypeStruct((b, l, w), ACT_DT),
                   jax.ShapeDtypeStruct(s0.shape, F32)],
        grid=(b, nblk),
        in_specs=[blk(fw), blk(bw), blk(fw), blk(bw), blk(fw), blk(bw),
                  pl.BlockSpec((2, w), lambda bi, j: (0, 0)),
                  pl.BlockSpec((1,) + s0.shape[1:], st)],
        out_specs=[blk(fw), blk(bw), pl.BlockSpec((1,) + s0.shape[1:], st)],
        scratch_shapes=[pltpu.VMEM(s0.shape[1:], F32)],
        compiler_params=_cparams("parallel", "arbitrary"),
        name="hgrn2_scan",
    )(hq, hq, hi, hi, ff, fb, lb, s0)


def _merge_kernel(x_ref, g1_ref, y5_ref, att_ref, of_ref, ob_ref, hg_ref, gate_ref,
                  wglu_ref, bglu_ref, hgn_ref, wbr_ref, wout_ref, gpost_ref, o_ref):
    d = x_ref.shape[-1]
    ge = jax.nn.gelu(y5_ref[0])
    ya = ge * jax.nn.sigmoid(jnp.dot(ge.astype(MXU_DT), wglu_ref[...], preferred_element_type=F32)
                             + bglu_ref[...])
    o = of_ref[0].astype(F32) + ob_ref[0].astype(F32)
    o = jnp.concatenate([_rms(o[:, h * HG_D:(h + 1) * HG_D]) for h in range(HG_HEADS)], axis=1)
    yc = o * hgn_ref[...] * hg_ref[0].astype(F32)
    ys = (ya.astype(MXU_DT), att_ref[0].astype(MXU_DT), yc.astype(MXU_DT))
    m = None
    for n in range(3):
        zn = jnp.dot(ys[n], wbr_ref[n], preferred_element_type=F32)
        term = gate_ref[0, :, n * d:(n + 1) * d].astype(F32) * zn
        m = term if m is None else m + term
    out = jnp.dot(m.astype(MXU_DT), wout_ref[...], preferred_element_type=F32)
    o_ref[0] = x_ref[0] + g1_ref[0] * (_rms(out) * gpost_ref[...])


def _merge(x, g1, y5, att, o_f, o_b, hg, gate, w_glu, b_glu, hg_norm, w_branch, w_out, li, g_post):
    b, l, d = x.shape
    tm = min(512, l)
    w = BRANCH_W
    row = lambda bi, i: (bi, i, 0)
    vec = lambda bi, i: (bi, 0, 0)
    c2 = lambda bi, i: (0, 0)
    return pl.pallas_call(
        _merge_kernel,
        out_shape=jax.ShapeDtypeStruct((b, l, d), F32),
        grid=(b, l // tm),
        in_specs=[pl.BlockSpec((1, tm, d), row),
                  pl.BlockSpec((1, 1, d), vec),
                  pl.BlockSpec((1, tm, w), row),
                  pl.BlockSpec((1, tm, w), row),
                  pl.BlockSpec((1, tm, w), row),
                  pl.BlockSpec((1, tm, w), row),
                  pl.BlockSpec((1, tm, w), row),
                  pl.BlockSpec((1, tm, 3 * d), row),
                  pl.BlockSpec((None, w, w), lambda bi, i: (li, 0, 0)),
                  pl.BlockSpec((1, w), c2),
                  pl.BlockSpec((1, w), c2),
                  pl.BlockSpec((None, 3, w, d), lambda bi, i: (li, 0, 0, 0)),
                  pl.BlockSpec((None, d, d), lambda bi, i: (li, 0, 0)),
                  pl.BlockSpec((1, d), c2)],
        out_specs=pl.BlockSpec((1, tm, d), row),
        compiler_params=_cparams("parallel", "parallel"),
        name="merge_out",
    )(x, g1, y5, att, o_f, o_b, hg, gate, w_glu, b_glu, hg_norm, w_branch, w_out, g_post)


def _ffn_kernel(x_ref, xp_ref, xn_ref, sh_ref, sc_ref, g2_ref, gpre_ref, gpost_ref,
                wup_ref, cw_ref, cb_ref, wd_ref, o_ref, h_ref, act_ref, *, nrow, tm, tn):
    i = pl.program_id(1)
    halo = SUBLANES
    f = wd_ref.shape[0]
    rows = tm + 2 * halo

    def prep(xx):
        hh = _rms(xx) * gpre_ref[...]
        return hh * (1.0 + sc_ref[0]) + sh_ref[0]

    h_ref[halo:halo + tm] = prep(x_ref[0]).astype(h_ref.dtype)
    h_ref[0:halo] = jnp.where(i > 0, prep(xp_ref[0]), 0.0).astype(h_ref.dtype)
    h_ref[halo + tm:rows] = jnp.where(i < nrow - 1, prep(xn_ref[0]), 0.0).astype(h_ref.dtype)

    def conv(c0):
        p = jnp.dot(h_ref[...], wup_ref[:, c0:c0 + tn], preferred_element_type=F32)
        up = pltpu.roll(p, 1, 0)[halo:halo + tm]
        dn = pltpu.roll(p, rows - 1, 0)[halo:halo + tm]
        cw = cw_ref[:, c0:c0 + tn]
        return cw[0:1] * up + cw[1:2] * p[halo:halo + tm] + cw[2:3] * dn + cb_ref[:, c0:c0 + tn]

    for jt in range(f // tn):
        a = conv(jt * tn)
        g = conv(f + jt * tn)
        act_ref[:, jt * tn:(jt + 1) * tn] = (a * jax.nn.sigmoid(a) * g).astype(act_ref.dtype)
    out = jnp.dot(act_ref[...], wd_ref[...], preferred_element_type=F32)
    o_ref[0] = x_ref[0] + g2_ref[0] * (_rms(out) * gpost_ref[...])


def _ffn(x, shift, scale, g2, g_pre, g_post, w_up, conv_w, conv_b, w_down, li):
    b, l, d = x.shape
    f = w_down.shape[1]
    tm = min(512, l)
    tn = 256
    nrow = l // tm
    hb = tm // SUBLANES
    row = lambda bi, i: (bi, i, 0)
    vec = lambda bi, i: (bi, 0, 0)
    c2 = lambda bi, i: (0, 0)
    resident = lambda shape: pl.BlockSpec((None,) + shape, lambda bi, i: (li, 0, 0),
                                          pipeline_mode=pl.Buffered(1))
    return pl.pallas_call(
        functools.partial(_ffn_kernel, nrow=nrow, tm=tm, tn=tn),
        out_shape=jax.ShapeDtypeStruct((b, l, d), F32),
        grid=(b, nrow),
        in_specs=[pl.BlockSpec((1, tm, d), row),
                  pl.BlockSpec((1, SUBLANES, d), lambda bi, i: (bi, jnp.maximum(i * hb - 1, 0), 0)),
                  pl.BlockSpec((1, SUBLANES, d), lambda bi, i: (bi, jnp.minimum((i + 1) * hb, l // SUBLANES - 1), 0)),
                  pl.BlockSpec((1, 1, d), vec),
                  pl.BlockSpec((1, 1, d), vec),
                  pl.BlockSpec((1, 1, d), vec),
                  pl.BlockSpec((1, d), c2),
                  pl.BlockSpec((1, d), c2),
                  resident((d, 2 * f)),
                  resident((CONV_W, 2 * f)),
                  resident((1, 2 * f)),
                  resident((f, d))],
        out_specs=pl.BlockSpec((1, tm, d), row),
        scratch_shapes=[pltpu.VMEM((tm + 2 * SUBLANES, d), MXU_DT),
                        pltpu.VMEM((tm, f), MXU_DT)],
        compiler_params=_cparams("parallel", "parallel"),
        name="conv_ffn",
    )(x, x, x, shift, scale, g2, g_pre, g_post, w_up, conv_w, conv_b, w_down)


def _rope_tables(l):
    rows = l // GRID_W
    row = jnp.repeat(jnp.arange(rows, dtype=F32), GRID_W)
    col = jnp.tile(jnp.arange(GRID_W, dtype=F32), rows)
    nf = HEAD_DIM // 4
    inv = ROPE_BASE ** (-jnp.arange(nf, dtype=F32) / nf)
    ang = jnp.concatenate([row[:, None] * inv, col[:, None] * inv], axis=-1)
    cos, sin = jnp.cos(ang), jnp.sin(ang)
    cos = jnp.tile(jnp.concatenate([cos, cos], axis=-1), (1, LANES // HEAD_DIM))
    sin = jnp.tile(jnp.concatenate([-sin, sin], axis=-1), (1, LANES // HEAD_DIM))
    return cos, sin


def _dup_kv_columns(w):
    hd = HEAD_DIM
    kv0 = 2 * BRANCH_W
    heads = [w[..., kv0 + i * hd:kv0 + (i + 1) * hd] for i in range(2 * ATT_KV_HEADS)]
    dup = [h for h in heads for _ in range(2)]
    return jnp.concatenate([w[..., :kv0]] + dup + [w[..., kv0 + 2 * ATT_KV_HEADS * hd:]], axis=-1)


def _lower_bounds(logits):
    pr = jax.nn.softmax(logits.astype(F32), axis=0)
    cs = jnp.cumsum(pr, axis=0)
    return cs - cs[:1]


def kernel(x, c, ctx, c_ctx, w_mod, b_mod, norm_g, w_in, s5_lam_re, s5_lam_im, s5_log_dt, s5_b_re, s5_b_im, s5_c_re, s5_c_im, s5_d, s5_w_glu, s5_b_glu, att_sink, hg_lb_logits, hg_norm_g, w_branch, w_out, ffn_w_up, ffn_conv_w, ffn_conv_b, ffn_w_down):
    b, l, d = x.shape
    lc = ctx.shape[1]
    depth = w_in.shape[0]
    nc = c.shape[0]
    nrows = -(-(nc + 1) // SUBLANES) * SUBLANES
    cond = jnp.zeros((nrows, d), F32).at[:nc].set(c).at[nc].set(c_ctx)
    mods = _mod_all(cond, w_mod, b_mod)
    cos_l, sin_l = _rope_tables(l)
    cos_c = jnp.ones((lc, LANES), F32)
    sin_c = jnp.zeros((lc, LANES), F32)
    lb_all = _lower_bounds(hg_lb_logits)
    tables_all = jax.vmap(_s5_tables)(s5_lam_re, s5_lam_im, s5_log_dt, s5_b_re, s5_b_im,
                                      s5_c_re, s5_c_im, s5_d)

    w_in_b = _dup_kv_columns(w_in.astype(MXU_DT))
    w_glu = s5_w_glu.astype(MXU_DT)
    wbr = w_branch.astype(MXU_DT)
    wout = w_out.astype(MXU_DT)
    wup = ffn_w_up.astype(MXU_DT)
    wdn = ffn_w_down.astype(MXU_DT)
    cw = ffn_conv_w.astype(F32)
    cb = ffn_conv_b.astype(F32).reshape(depth, 1, -1)

    xl, xc = x, ctx
    for li in range(depth):
        with_ctx_out = li < depth - 1
        ml = mods[li, :nc].reshape(nc, 1, 6 * d)
        mc = jnp.broadcast_to(mods[li, nc].reshape(1, 1, 6 * d), (b, 1, 6 * d))
        sl = lambda m, k: m[:, :, k * d:(k + 1) * d]
        gains = norm_g[li].astype(F32)
        sink = att_sink[li].astype(F32) * LOG2E
        b_glu = s5_b_glu[li].astype(F32).reshape(1, BRANCH_W)
        hgn = hg_norm_g[li].astype(F32).reshape(1, BRANCH_W)
        lb = lb_all[li]

        pc = _win(xc, sl(mc, 0), sl(mc, 1), gains[0:1], w_in_b, li, cos_c, sin_c)
        u_c, q_c, k_c, v_c, hq_c, ff_c, fb_c, hi_c, hg_c, gate_c = pc
        zero_h = jnp.zeros((b, S5_NQ, 1, 4 * S5_SW), F32)
        y5_c, h_ctx = _s5(u_c, tables_all, li, zero_h)
        zero_s = jnp.zeros((b, 2, HG_HEADS, HG_D, HG_D), F32)
        of_c, ob_c, s_ctx = _hgrn(hq_c, hi_c, ff_c, fb_c, lb, zero_s)

        pl_ = _win(xl, sl(ml, 0), sl(ml, 1), gains[0:1], w_in_b, li, cos_l, sin_l)
        u_l, q_l, k_l, v_l, hq_l, ff_l, fb_l, hi_l, hg_l, gate_l = pl_
        y5_l, _ = _s5(u_l, tables_all, li, h_ctx)
        att_l = _attn(q_l, k_l, v_l, k_c, v_c, sink)
        of_l, ob_l, _ = _hgrn(hq_l, hi_l, ff_l, fb_l, lb, s_ctx)
        xl = _merge(xl, sl(ml, 2), y5_l, att_l, of_l, ob_l, hg_l, gate_l,
                    w_glu, b_glu, hgn, wbr, wout, li, gains[1:2])
        xl = _ffn(xl, sl(ml, 3), sl(ml, 4), sl(ml, 5), gains[2:3], gains[3:4], wup, cw, cb, wdn, li)

        if with_ctx_out:
            att_c = _attn_ctx(q_c, k_c, v_c, sink)
            xc = _merge(xc, sl(mc, 2), y5_c, att_c, of_c, ob_c, hg_c, gate_c,
                        w_glu, b_glu, hgn, wbr, wout, li, gains[1:2])
            xc = _ffn(xc, sl(mc, 3), sl(mc, 4), sl(mc, 5), gains[2:3], gains[3:4], wup, cw, cb, wdn, li)
    return xl
```

```python
import functools
import math

import jax
import jax.numpy as jnp
from jax import lax
from jax.experimental import pallas as pl
from jax.experimental.pallas import tpu as pltpu

F32 = jnp.float32
MXU_DT = jnp.bfloat16
ACT_DT = jnp.bfloat16
EPS = 1e-6

BRANCH_W = 512
S5_GROUP = 16
S5_GROUPS = BRANCH_W // S5_GROUP
S5_STATE = 64
HEAD_DIM = 64
ATT_HEADS = 8
ATT_KV_HEADS = 2
ATT_GRP = ATT_HEADS // ATT_KV_HEADS
ATT_BLOCK = 128
GRID_W = 64
ROPE_BASE = 10000.0
HG_HEADS = 4
HG_D = 128
CONV_W = 3

LANES = 128
SUBLANES = 8
VMEM_LIMIT = 56 * 1024 * 1024

S5_T = 8
S5_GPT = LANES // S5_GROUP
S5_NQ = BRANCH_W // LANES
S5_SW = S5_GPT * S5_STATE
S5_MAX_ROWS = 512
HG_CHUNK = 64
LOG2E = math.log2(math.e)
HG_MID = HG_CHUNK // 2


def _cparams(*sem):
    return pltpu.CompilerParams(dimension_semantics=sem, vmem_limit_bytes=VMEM_LIMIT)


def _rms(x):
    return x * lax.rsqrt(jnp.mean(x * x, axis=-1, keepdims=True) + EPS)


def _mod_kernel(a_ref, w_ref, b_ref, o_ref):
    a = a_ref[...]
    a = a * jax.nn.sigmoid(a)
    o_ref[0] = lax.dot_general(a, w_ref[0], (((1,), (0,)), ((), ())),
                               precision=lax.Precision.HIGHEST,
                               preferred_element_type=F32) + b_ref[0]


def _mod_all(cond, w_mod, b_mod):
    depth, d, n = w_mod.shape
    r = cond.shape[0]
    tn = 1536
    return pl.pallas_call(
        _mod_kernel,
        out_shape=jax.ShapeDtypeStruct((depth, r, n), F32),
        grid=(depth, n // tn),
        in_specs=[pl.BlockSpec((r, d), lambda l, j: (0, 0)),
                  pl.BlockSpec((1, d, tn), lambda l, j: (l, 0, j)),
                  pl.BlockSpec((1, 1, tn), lambda l, j: (l, 0, j))],
        out_specs=pl.BlockSpec((1, r, tn), lambda l, j: (l, 0, j)),
        compiler_params=_cparams("parallel", "parallel"),
        name="adaln_mod",
    )(cond, w_mod, b_mod.reshape(depth, 1, n))


_C_U = (0, 512)
_C_Q = (512, 1024)
_C_K = (1024, 1152)
_C_V = (1152, 1280)
_C_HQ = (1280, 1792)
_C_FF = (1792, 2304)
_C_FB = (2304, 2816)
_C_HI = (2816, 3328)
_C_HG = (3328, 3840)
_C_GATE = (3840, 6912)
KV_W = 2 * ATT_KV_HEADS * HEAD_DIM


def _rope(z, cos, sin):
    lane = lax.broadcasted_iota(jnp.int32, z.shape, 1)
    first = (lane & (HEAD_DIM // 2)) == 0
    partner = jnp.where(first, pltpu.roll(z, LANES - HEAD_DIM // 2, 1), pltpu.roll(z, HEAD_DIM // 2, 1))
    return z * cos + partner * sin


def _win_kernel(x_ref, sh_ref, sc_ref, g_ref, w_ref, cos_ref, sin_ref,
                u_ref, q_ref, k_ref, v_ref, hq_ref, ff_ref, fb_ref, hi_ref, hg_ref, gate_ref):
    x = x_ref[0]
    h = _rms(x) * g_ref[...]
    h = h * (1.0 + sc_ref[0]) + sh_ref[0]
    hb = h.astype(MXU_DT)

    def mm(lo, hi):
        return jnp.dot(hb, w_ref[:, lo:hi], preferred_element_type=F32)

    u_ref[0] = mm(*_C_U)
    cos = cos_ref[...]
    sin = sin_ref[...]
    zq = mm(*_C_Q)
    for s in range(BRANCH_W // LANES):
        z = _rope(zq[:, s * LANES:(s + 1) * LANES], cos, sin) * (HEAD_DIM ** -0.5 * LOG2E)
        q_ref[0, :, s * LANES:(s + 1) * LANES] = z.astype(q_ref.dtype)
    zkv = mm(_C_K[0], _C_V[1])
    lane = lax.broadcasted_iota(jnp.int32, (1, LANES), 1)
    lo = lane < HEAD_DIM
    for ref, z in ((k_ref, _rope(zkv[:, :LANES], cos, sin)), (v_ref, zkv[:, LANES:])):
        swapped = pltpu.roll(z, HEAD_DIM, 1)
        ref[0, :, :LANES] = jnp.where(lo, z, swapped).astype(ref.dtype)
        ref[0, :, LANES:] = jnp.where(lo, swapped, z).astype(ref.dtype)
    z = mm(*_C_HQ)
    hq_ref[0] = (z * jax.nn.sigmoid(z)).astype(hq_ref.dtype)
    ff_ref[0] = mm(*_C_FF)
    fb_ref[0] = mm(*_C_FB)
    hi_ref[0] = mm(*_C_HI).astype(hi_ref.dtype)
    hg_ref[0] = jax.nn.sigmoid(mm(*_C_HG)).astype(hg_ref.dtype)
    for s in range((_C_GATE[1] - _C_GATE[0]) // BRANCH_W):
        lo = _C_GATE[0] + s * BRANCH_W
        gate_ref[0, :, s * BRANCH_W:(s + 1) * BRANCH_W] = (
            jax.nn.sigmoid(mm(lo, lo + BRANCH_W)).astype(gate_ref.dtype))


def _win(x, shift, scale, gain, w, li, cos, sin):
    b, l, d = x.shape
    tm = min(512, l)
    n = w.shape[-1]
    widths = [(512, F32), (512, ACT_DT), (KV_W, ACT_DT), (KV_W, ACT_DT), (512, ACT_DT),
              (512, F32), (512, F32), (512, ACT_DT), (512, ACT_DT), (3072, ACT_DT)]
    row = lambda bi, i: (bi, i, 0)
    vec = lambda bi, i: (bi, 0, 0)
    return pl.pallas_call(
        _win_kernel,
        out_shape=[jax.ShapeDtypeStruct((b, l, wd), dt) for wd, dt in widths],
        grid=(b, l // tm),
        in_specs=[pl.BlockSpec((1, tm, d), row),
                  pl.BlockSpec((1, 1, d), vec),
                  pl.BlockSpec((1, 1, d), vec),
                  pl.BlockSpec((1, d), lambda bi, i: (0, 0)),
                  pl.BlockSpec((None, d, n), lambda bi, i: (li, 0, 0), pipeline_mode=pl.Buffered(1)),
                  pl.BlockSpec((tm, LANES), lambda bi, i: (i, 0)),
                  pl.BlockSpec((tm, LANES), lambda bi, i: (i, 0))],
        out_specs=[pl.BlockSpec((1, tm, wd), row) for wd, _ in widths],
        compiler_params=_cparams("parallel", "parallel"),
        name="in_proj",
    )(x, shift, scale, gain, w, cos, sin)


def _s5_tables(lam_re, lam_im, log_dt, b_re, b_im, c_re, c_im, d_skip):
    t_, g_, n_, c_ = S5_T, S5_GROUPS, S5_STATE, S5_GROUP
    nq, gpt, sw = S5_NQ, S5_GPT, S5_SW
    hp = lax.Precision.HIGHEST
    lr = lam_re.astype(F32)
    li = lam_im.astype(F32)
    dt = jnp.exp(log_dt.astype(F32))[..., None]
    xr, xi = dt * lr, dt * li
    mag = jnp.exp(xr)
    ar, ai = mag * jnp.cos(xi), mag * jnp.sin(xi)
    den = lr * lr + li * li
    fr = ((ar - 1.0) * lr + ai * li) / den
    fi = (ai * lr - (ar - 1.0) * li) / den
    br, bi = b_re.astype(F32), b_im.astype(F32)
    bbr = fr[..., None] * br - fi[..., None] * bi
    bbi = fr[..., None] * bi + fi[..., None] * br
    kk = jnp.arange(t_ + 1, dtype=F32)[:, None, None, None]
    pr = jnp.exp(kk * xr) * jnp.cos(kk * xi)
    pi = jnp.exp(kk * xr) * jnp.sin(kk * xi)
    wr = pr[..., None] * bbr - pi[..., None] * bbi
    wi = pr[..., None] * bbi + pi[..., None] * bbr
    cr, ci = c_re.astype(F32), c_im.astype(F32)
    kern = (jnp.einsum('gon,kdgni->kdgoi', cr, wr[:t_], precision=hp)
            - jnp.einsum('gon,kdgni->kdgoi', ci, wi[:t_], precision=hp))
    s_idx = jnp.arange(t_)[:, None]
    t_idx = jnp.arange(t_)[None, :]
    sel = lambda m: m[..., None, None, None]
    skip = jnp.eye(c_, dtype=F32) * d_skip.astype(F32).reshape(g_, 1, c_)
    kst = (jnp.where(sel(t_idx >= s_idx), kern[jnp.clip(t_idx - s_idx, 0, t_ - 1), 0], 0.0)
           + jnp.where(sel(s_idx >= t_idx), kern[jnp.clip(s_idx - t_idx, 0, t_ - 1), 1], 0.0)
           + jnp.where(sel(s_idx == t_idx), skip, 0.0))
    toe_c = jnp.transpose(kst.reshape(t_, t_, nq, gpt, c_, c_), (2, 0, 3, 5, 1, 4))
    toe_c = toe_c.reshape(nq, t_ * LANES, t_ * c_)
    q6 = jnp.stack([jnp.stack([wr[:t_, 0][::-1], wi[:t_, 0][::-1]], 0),
                    jnp.stack([wr[:t_, 1], wi[:t_, 1]], 0)], 0)
    qm_c = jnp.transpose(q6.reshape(2, 2, t_, nq, gpt, n_, c_), (3, 2, 4, 6, 0, 1, 5))
    qm_c = qm_c.reshape(nq, t_ * LANES, 4 * n_)
    psr = jnp.stack([pr[1:, 0], pr[1:, 1][::-1]], 0)[..., None]
    psi = jnp.stack([pi[1:, 0], pi[1:, 1][::-1]], 0)[..., None]
    crt = jnp.swapaxes(cr, 1, 2)
    cit = jnp.swapaxes(ci, 1, 2)
    p6 = jnp.stack([crt * psr - cit * psi, -crt * psi - cit * psr], 1)
    pm_c = jnp.transpose(p6.reshape(2, 2, t_, nq, gpt, n_, c_), (3, 0, 1, 4, 5, 2, 6))
    pm_c = pm_c.reshape(nq, 4 * sw, t_ * c_)

    def expand(tab, rep_rows, rep_cols, inner):
        rows, cols = tab.shape[1], tab.shape[2]
        src = lax.broadcasted_iota(jnp.int32, (cols, cols * gpt), 0)
        dst = lax.broadcasted_iota(jnp.int32, (cols, cols * gpt), 1)
        rep = ((src // inner == dst // (inner * gpt)) & (src % inner == dst % inner)).astype(MXU_DT)
        out = jnp.einsum('qrc,cd->qrd', tab.astype(MXU_DT), rep, preferred_element_type=MXU_DT)
        rg = lax.broadcasted_iota(jnp.int32, (rows, cols * gpt), 0) // rep_rows % gpt
        cg = lax.broadcasted_iota(jnp.int32, (rows, cols * gpt), 1) // rep_cols % gpt
        return jnp.where(rg == cg, out, jnp.zeros_like(out))

    toe = expand(toe_c, c_, c_, c_)
    qm = expand(qm_c, c_, n_, n_)
    pm = expand(pm_c, n_, c_, c_)
    dsel = jnp.stack([pr[t_], pi[t_]], 1).reshape(2, 2, nq, sw)
    dec = jnp.transpose(dsel, (2, 0, 1, 3)).reshape(nq, 1, 4 * sw)
    return toe, qm, pm, dec


def _s5_kernel(u_ref, toe_ref, qm_ref, pm_ref, dec_ref, h0_ref, y_ref, hend_ref,
               z_ref, s_ref, hin_ref, *, nj, bpb):
    t_, sw = S5_T, S5_SW
    for bi in range(bpb):
        for t in range(t_):
            z_ref[bi * nj:(bi + 1) * nj, t * LANES:(t + 1) * LANES] = (
                u_ref[bi, pl.ds(t, nj, stride=t_), :].astype(z_ref.dtype))
    tiled = bpb > 1
    cpc = sw // LANES
    inc = jnp.dot(z_ref[...], qm_ref[0], preferred_element_type=F32)
    if tiled:
        for c in range(4 * cpc):
            s_ref[c] = inc[:, c * LANES:(c + 1) * LANES]
    else:
        s_ref[...] = inc
    dec = dec_ref[0]
    dfr, dfi, dbr, dbi = (dec[:, i * sw:(i + 1) * sw] for i in range(4))
    h0 = h0_ref[:, 0, 0, :]

    def read(ref, comp, rows):
        if not tiled:
            return ref[rows, comp * sw:(comp + 1) * sw]
        return jnp.concatenate([ref[comp * cpc + k, rows, :] for k in range(cpc)], axis=1)

    def write(ref, comp, rows, val):
        if not tiled:
            ref[rows, comp * sw:(comp + 1) * sw] = val
            return
        for k in range(cpc):
            ref[comp * cpc + k, rows, :] = val[:, k * LANES:(k + 1) * LANES]

    def step(j, carry):
        fr, fi, br, bi_ = carry
        rows = (lambda r: pl.ds(r, bpb, stride=nj)) if tiled else (lambda r: pl.ds(r, 1))
        fw = rows(j)
        bw = rows(nj - 1 - j)
        write(hin_ref, 0, fw, fr)
        write(hin_ref, 1, fw, fi)
        write(hin_ref, 2, bw, br)
        write(hin_ref, 3, bw, bi_)
        sfr, sfi = read(s_ref, 0, fw), read(s_ref, 1, fw)
        sbr, sbi = read(s_ref, 2, bw), read(s_ref, 3, bw)
        return (dfr * fr - dfi * fi + sfr, dfr * fi + dfi * fr + sfi,
                dbr * br - dbi * bi_ + sbr, dbr * bi_ + dbi * br + sbi)

    fin = lax.fori_loop(0, nj, step, tuple(h0[:, i * sw:(i + 1) * sw] for i in range(4)), unroll=8)
    for i in range(4):
        hend_ref[:, 0, 0, i * sw:(i + 1) * sw] = fin[i]
    hin = jnp.concatenate([hin_ref[c] for c in range(4 * cpc)], axis=1) if tiled else hin_ref[...]
    y = (jnp.dot(z_ref[...], toe_ref[0], preferred_element_type=F32)
         + jnp.dot(hin.astype(MXU_DT), pm_ref[0], preferred_element_type=F32))
    for bi in range(bpb):
        for t in range(t_):
            y_ref[bi, pl.ds(t, nj, stride=t_), :] = y[bi * nj:(bi + 1) * nj, t * LANES:(t + 1) * LANES]


def _s5(u, tables, li, h0):
    toe, qm, pm, dec = tables
    b, l, _ = u.shape
    nj = l // S5_T
    bpb = b if b * nj <= S5_MAX_ROWS else 1
    kw = S5_T * LANES
    sw4 = 4 * S5_SW
    wmap = lambda q, bi: (li, q, 0, 0)
    state_shape = (sw4 // LANES, bpb * nj, LANES) if bpb > 1 else (nj, sw4)
    return pl.pallas_call(
        functools.partial(_s5_kernel, nj=nj, bpb=bpb),
        out_shape=[jax.ShapeDtypeStruct((b, l, BRANCH_W), F32),
                   jax.ShapeDtypeStruct((b, S5_NQ, 1, sw4), F32)],
        grid=(S5_NQ, b // bpb),
        in_specs=[pl.BlockSpec((bpb, l, LANES), lambda q, bi: (bi, 0, q)),
                  pl.BlockSpec((None, 1, kw, kw), wmap),
                  pl.BlockSpec((None, 1, kw, sw4), wmap),
                  pl.BlockSpec((None, 1, sw4, kw), wmap),
                  pl.BlockSpec((None, 1, 1, sw4), wmap),
                  pl.BlockSpec((bpb, 1, 1, sw4), lambda q, bi: (bi, q, 0, 0))],
        out_specs=[pl.BlockSpec((bpb, l, LANES), lambda q, bi: (bi, 0, q)),
                   pl.BlockSpec((bpb, 1, 1, sw4), lambda q, bi: (bi, q, 0, 0))],
        scratch_shapes=[pltpu.VMEM((bpb * nj, kw), MXU_DT),
                        pltpu.VMEM(state_shape, F32),
                        pltpu.VMEM(state_shape, F32)],
        compiler_params=_cparams("parallel", "parallel"),
        name="s5_mix",
    )(u, toe, qm, pm, dec, h0)


def _softmax_pv(s, sink_col, v):
    m = jnp.maximum(jnp.max(s, axis=-1, keepdims=True), sink_col)
    p = jnp.exp2(s - m)
    den = jnp.sum(p, axis=-1, keepdims=True) + jnp.exp2(sink_col - m)
    o = jnp.dot(p.astype(MXU_DT), v, preferred_element_type=F32)
    return o / den


def _attn_group(sink_ref, q_ref, r0, hk, kcat, vcat, bias, rows):
    half = ATT_GRP // 2
    lane = lax.broadcasted_iota(jnp.int32, (1, LANES), 1)
    lo = lane < HEAD_DIM
    k_lo = kcat * lo.astype(kcat.dtype)
    k_hi = kcat * (~lo).astype(kcat.dtype)
    qs = jnp.concatenate([q_ref[0, r0:r0 + rows, (hk * half + a) * LANES:(hk * half + a + 1) * LANES]
                          for a in range(half)], axis=0)
    dn = (((1,), (1,)), ((), ()))
    s = jnp.concatenate([lax.dot_general(qs, k_lo, dn, preferred_element_type=F32),
                         lax.dot_general(qs, k_hi, dn, preferred_element_type=F32)], axis=0)
    heads = [hk * ATT_GRP + 2 * a for a in range(half)] + [hk * ATT_GRP + 2 * a + 1 for a in range(half)]
    sink_col = jnp.concatenate([jnp.full((rows, 1), sink_ref[h], F32) for h in heads], axis=0)
    if bias is not None:
        s = s + jnp.concatenate([bias] * ATT_GRP, axis=0)
    o = _softmax_pv(s, sink_col, vcat)
    return [jnp.where(lo, o[a * rows:(a + 1) * rows], o[(half + a) * rows:(half + a + 1) * rows])
            for a in range(half)]


def _attn_kernel(sink_ref, q_ref, kp_ref, kc_ref, kn_ref, vp_ref, vc_ref, vn_ref, kx_ref, vx_ref,
                 o_ref, *, nsteps, lc, qb):
    i = pl.program_id(1)
    blk = ATT_BLOCK
    neg = jnp.float32(-jnp.inf)
    qi = lax.broadcasted_iota(jnp.int32, (blk, blk), 0)
    kj = lax.broadcasted_iota(jnp.int32, (blk, blk), 1)
    win_p = jnp.where(kj >= qi, 0.0, neg)
    win_n = jnp.where(kj <= qi, 0.0, neg)
    edge_p = jnp.where(i > 0, win_p, neg)
    edge_n = jnp.where(i < nsteps - 1, win_n, neg)
    zc = jnp.zeros((blk, lc), F32)
    zb = jnp.zeros((blk, blk), F32)
    half = ATT_GRP // 2
    for hk in range(ATT_KV_HEADS):
        ks = slice(hk * LANES, (hk + 1) * LANES)
        kblocks = ([kp_ref[0, :, ks]] + [kc_ref[0, sb * blk:(sb + 1) * blk, ks] for sb in range(qb)]
                   + [kn_ref[0, :, ks]])
        vblocks = ([vp_ref[0, :, ks]] + [vc_ref[0, sb * blk:(sb + 1) * blk, ks] for sb in range(qb)]
                   + [vn_ref[0, :, ks]])
        for sb in range(qb):
            kcat = jnp.concatenate([kx_ref[0, :, ks]] + kblocks[sb:sb + 3], axis=0)
            vcat = jnp.concatenate([vx_ref[0, :, ks]] + vblocks[sb:sb + 3], axis=0)
            bias = jnp.concatenate([zc, edge_p if sb == 0 else win_p, zb,
                                    edge_n if sb == qb - 1 else win_n], axis=1)
            for a, slab in enumerate(_attn_group(sink_ref, q_ref, sb * blk, hk, kcat, vcat, bias, blk)):
                c0 = (hk * half + a) * LANES
                o_ref[0, sb * blk:(sb + 1) * blk, c0:c0 + LANES] = slab.astype(o_ref.dtype)


def _attn(q, k, v, kx, vx, sink):
    b, l, _ = q.shape
    lc = kx.shape[1]
    nb = l // ATT_BLOCK
    qb = 2 if nb % 2 == 0 else 1
    nsteps = nb // qb
    cur = lambda bi, i: (bi, i, 0)
    prev = lambda bi, i: (bi, jnp.maximum(qb * i - 1, 0), 0)
    nxt = lambda bi, i: (bi, jnp.minimum(qb * (i + 1), nb - 1), 0)
    ctx = lambda bi, i: (bi, 0, 0)
    edge = lambda m: pl.BlockSpec((1, ATT_BLOCK, KV_W), m)
    body = pl.BlockSpec((1, qb * ATT_BLOCK, KV_W), cur)
    return pl.pallas_call(
        functools.partial(_attn_kernel, nsteps=nsteps, lc=lc, qb=qb),
        out_shape=jax.ShapeDtypeStruct((b, l, BRANCH_W), ACT_DT),
        grid=(b, nsteps),
        in_specs=[pl.BlockSpec(memory_space=pltpu.SMEM),
                  pl.BlockSpec((1, qb * ATT_BLOCK, BRANCH_W), cur),
                  edge(prev), body, edge(nxt),
                  edge(prev), body, edge(nxt),
                  pl.BlockSpec((1, lc, KV_W), ctx),
                  pl.BlockSpec((1, lc, KV_W), ctx)],
        out_specs=pl.BlockSpec((1, qb * ATT_BLOCK, BRANCH_W), cur),
        compiler_params=_cparams("parallel", "parallel"),
        name="window_attn",
    )(sink, q, k, k, k, v, v, v, kx, vx)


def _attn_ctx_kernel(sink_ref, q_ref, k_ref, v_ref, o_ref, *, lc):
    half = ATT_GRP // 2
    for hk in range(ATT_KV_HEADS):
        ks = slice(hk * LANES, (hk + 1) * LANES)
        for a, slab in enumerate(_attn_group(sink_ref, q_ref, 0, hk, k_ref[0, :, ks], v_ref[0, :, ks], None, lc)):
            c0 = (hk * half + a) * LANES
            o_ref[0, :, c0:c0 + LANES] = slab.astype(o_ref.dtype)


def _attn_ctx(q, k, v, sink):
    b, lc, _ = q.shape
    kvw = KV_W
    full = lambda bi: (bi, 0, 0)
    return pl.pallas_call(
        functools.partial(_attn_ctx_kernel, lc=lc),
        out_shape=jax.ShapeDtypeStruct((b, lc, BRANCH_W), ACT_DT),
        grid=(b,),
        in_specs=[pl.BlockSpec(memory_space=pltpu.SMEM),
                  pl.BlockSpec((1, lc, BRANCH_W), full),
                  pl.BlockSpec((1, lc, kvw), full),
                  pl.BlockSpec((1, lc, kvw), full)],
        out_specs=pl.BlockSpec((1, lc, BRANCH_W), full),
        compiler_params=_cparams("parallel"),
        name="ctx_attn",
    )(sink, q, k, v)


def _split2(x):
    a = x.astype(jnp.bfloat16)
    b = (x - a.astype(F32)).astype(jnp.bfloat16)
    return a, b


def _hg_block(q, z, v, lb, tri, diag, st_ref, d, reverse, cpb):
    c = HG_CHUNK
    w = q.shape[-1]
    s = jax.nn.sigmoid(z)
    f = lb + (1.0 - lb) * s
    logf = jnp.log(f)
    kf = (1.0 - lb) * (1.0 - s)
    cum = sum(jnp.dot(tri, part, preferred_element_type=F32) for part in _split2(logf))
    end_off = 0 if reverse else c - 1
    mid_off = HG_MID if reverse else HG_MID - 1
    cum_end = [cum[ci * c + end_off:ci * c + end_off + 1] for ci in range(cpb)]
    cum_mid = [cum[ci * c + mid_off:ci * c + mid_off + 1] for ci in range(cpb)]
    rows = lambda parts: jnp.concatenate([jnp.broadcast_to(p, (c, w)) for p in parts], axis=0)
    rel = cum - rows(cum_mid)
    qf = q.astype(F32) * jnp.exp(rel)
    kk = kf * jnp.exp(-rel)
    q_in = qf.astype(MXU_DT)
    k_in = kk.astype(MXU_DT)
    q_st = (qf * rows([jnp.exp(m) for m in cum_mid])).astype(MXU_DT)
    k_st = (kk * rows([jnp.exp(e - m) for e, m in zip(cum_end, cum_mid)])).astype(MXU_DT)
    dec = [jnp.exp(e) for e in cum_end]
    vb = v.astype(MXU_DT)
    order = range(cpb - 1, -1, -1) if reverse else range(cpb)
    outs = []
    for h in range(HG_HEADS):
        hs = slice(h * HG_D, (h + 1) * HG_D)
        att = lax.dot_general(q_in[:, hs], k_in[:, hs], (((1,), (1,)), ((), ())), preferred_element_type=F32)
        att = jnp.where(tri > 0, att, 0.0).astype(MXU_DT)
        o = jnp.dot(att, vb[:, hs], preferred_element_type=F32)
        k_bd = jnp.concatenate([k_st[:, hs]] * cpb, axis=1) * diag
        upd = lax.dot_general(vb[:, hs], k_bd, (((0,), (0,)), ((), ())), preferred_element_type=F32)
        st = st_ref[d, h]
        entering = [None] * cpb
        for ci in order:
            entering[ci] = st
            st = st * dec[ci][:, hs] + upd[:, ci * HG_D:(ci + 1) * HG_D]
        st_ref[d, h] = st
        s_cat = jnp.concatenate(entering, axis=1).astype(MXU_DT)
        q_bd = jnp.concatenate([q_st[:, hs]] * cpb, axis=1) * diag
        o = o + lax.dot_general(q_bd, s_cat, (((1,), (1,)), ((), ())), preferred_element_type=F32)
        outs.append(o)
    return jnp.concatenate(outs, axis=1)


def _hgrn_kernel(qf_ref, qb_ref, vf_ref, vb_ref, ff_ref, fb_ref, lb_ref, s0_ref,
                 of_ref, ob_ref, send_ref, st_ref, *, nblk, cpb):
    j = pl.program_id(1)
    c = HG_CHUNK
    tb = cpb * c

    @pl.when(j == 0)
    def _():
        st_ref[...] = s0_ref[0]

    r = lax.broadcasted_iota(jnp.int32, (tb, tb), 0)
    col = lax.broadcasted_iota(jnp.int32, (tb, tb), 1)
    same = (r // c) == (col // c)
    tri_f = (same & (r >= col)).astype(MXU_DT)
    tri_b = (same & (r <= col)).astype(MXU_DT)
    dr = lax.broadcasted_iota(jnp.int32, (tb, cpb * HG_D), 0)
    dc = lax.broadcasted_iota(jnp.int32, (tb, cpb * HG_D), 1)
    diag = ((dr // c) == (dc // HG_D)).astype(MXU_DT)
    of_ref[0] = _hg_block(qf_ref[0], ff_ref[0], vf_ref[0], lb_ref[0:1], tri_f, diag, st_ref, 0,
                          False, cpb).astype(of_ref.dtype)
    ob_ref[0] = _hg_block(qb_ref[0], fb_ref[0], vb_ref[0], lb_ref[1:2], tri_b, diag, st_ref, 1,
                          True, cpb).astype(ob_ref.dtype)

    @pl.when(j == nblk - 1)
    def _():
        send_ref[0] = st_ref[...]


def _hgrn(hq, hi, ff, fb, lb, s0):
    b, l, w = hq.shape
    tb = min(256, l)
    nblk = l // tb
    fw = lambda bi, j: (bi, j, 0)
    bw = lambda bi, j: (bi, nblk - 1 - j, 0)
    st = lambda bi, j: (bi, 0, 0, 0, 0)
    blk = lambda m: pl.BlockSpec((1, tb, w), m)
    return pl.pallas_call(
        functools.partial(_hgrn_kernel, nblk=nblk, cpb=tb // HG_CHUNK),
        out_shape=[jax.ShapeDtypeStruct((b, l, w), ACT_DT),
                   jax.ShapeDtypeStruct((b, l, w), ACT_DT),
                   jax.ShapeDtypeStruct(s0.shape, F32)],
        grid=(b, nblk),
        in_specs=[blk(fw), blk(bw), blk(fw), blk(bw), blk(fw), blk(bw),
                  pl.BlockSpec((2, w), lambda bi, j: (0, 0)),
                  pl.BlockSpec((1,) + s0.shape[1:], st)],
        out_specs=[blk(fw), blk(bw), pl.BlockSpec((1,) + s0.shape[1:], st)],
        scratch_shapes=[pltpu.VMEM(s0.shape[1:], F32)],
        compiler_params=_cparams("parallel", "arbitrary"),
        name="hgrn2_scan",
    )(hq, hq, hi, hi, ff, fb, lb, s0)


def _merge_kernel(x_ref, g1_ref, y5_ref, att_ref, of_ref, ob_ref, hg_ref, gate_ref,
                  wglu_ref, bglu_ref, hgn_ref, wbr_ref, wout_ref, gpost_ref, o_ref):
    d = x_ref.shape[-1]
    ge = jax.nn.gelu(y5_ref[0])
    ya = ge * jax.nn.sigmoid(jnp.dot(ge.astype(MXU_DT), wglu_ref[...], preferred_element_type=F32)
                             + bglu_ref[...])
    o = of_ref[0].astype(F32) + ob_ref[0].astype(F32)
    o = jnp.concatenate([_rms(o[:, h * HG_D:(h + 1) * HG_D]) for h in range(HG_HEADS)], axis=1)
    yc = o * hgn_ref[...] * hg_ref[0].astype(F32)
    ys = (ya.astype(MXU_DT), att_ref[0].astype(MXU_DT), yc.astype(MXU_DT))
    m = None
    for n in range(3):
        zn = jnp.dot(ys[n], wbr_ref[n], preferred_element_type=F32)
        term = gate_ref[0, :, n * d:(n + 1) * d].astype(F32) * zn
        m = term if m is None else m + term
    out = jnp.dot(m.astype(MXU_DT), wout_ref[...], preferred_element_type=F32)
    o_ref[0] = x_ref[0] + g1_ref[0] * (_rms(out) * gpost_ref[...])


def _merge(x, g1, y5, att, o_f, o_b, hg, gate, w_glu, b_glu, hg_norm, w_branch, w_out, li, g_post):
    b, l, d = x.shape
    tm = min(512, l)
    w = BRANCH_W
    row = lambda bi, i: (bi, i, 0)
    vec = lambda bi, i: (bi, 0, 0)
    c2 = lambda bi, i: (0, 0)
    return pl.pallas_call(
        _merge_kernel,
        out_shape=jax.ShapeDtypeStruct((b, l, d), F32),
        grid=(b, l // tm),
        in_specs=[pl.BlockSpec((1, tm, d), row),
                  pl.BlockSpec((1, 1, d), vec),
                  pl.BlockSpec((1, tm, w), row),
                  pl.BlockSpec((1, tm, w), row),
                  pl.BlockSpec((1, tm, w), row),
                  pl.BlockSpec((1, tm, w), row),
                  pl.BlockSpec((1, tm, w), row),
                  pl.BlockSpec((1, tm, 3 * d), row),
                  pl.BlockSpec((None, w, w), lambda bi, i: (li, 0, 0)),
                  pl.BlockSpec((1, w), c2),
                  pl.BlockSpec((1, w), c2),
                  pl.BlockSpec((None, 3, w, d), lambda bi, i: (li, 0, 0, 0)),
                  pl.BlockSpec((None, d, d), lambda bi, i: (li, 0, 0)),
                  pl.BlockSpec((1, d), c2)],
        out_specs=pl.BlockSpec((1, tm, d), row),
        compiler_params=_cparams("parallel", "parallel"),
        name="merge_out",
    )(x, g1, y5, att, o_f, o_b, hg, gate, w_glu, b_glu, hg_norm, w_branch, w_out, g_post)


def _ffn_kernel(x_ref, xp_ref, xn_ref, sh_ref, sc_ref, g2_ref, gpre_ref, gpost_ref,
                wup_ref, cw_ref, cb_ref, wd_ref, o_ref, h_ref, act_ref, *, nrow, tm, tn):
    i = pl.program_id(1)
    halo = SUBLANES
    f = wd_ref.shape[0]
    rows = tm + 2 * halo

    def prep(xx):
        hh = _rms(xx) * gpre_ref[...]
        return hh * (1.0 + sc_ref[0]) + sh_ref[0]

    h_ref[halo:halo + tm] = prep(x_ref[0]).astype(h_ref.dtype)
    h_ref[0:halo] = jnp.where(i > 0, prep(xp_ref[0]), 0.0).astype(h_ref.dtype)
    h_ref[halo + tm:rows] = jnp.where(i < nrow - 1, prep(xn_ref[0]), 0.0).astype(h_ref.dtype)

    def conv(c0):
        p = jnp.dot(h_ref[...], wup_ref[:, c0:c0 + tn], preferred_element_type=F32)
        up = pltpu.roll(p, 1, 0)[halo:halo + tm]
        dn = pltpu.roll(p, rows - 1, 0)[halo:halo + tm]
        cw = cw_ref[:, c0:c0 + tn]
        return cw[0:1] * up + cw[1:2] * p[halo:halo + tm] + cw[2:3] * dn + cb_ref[:, c0:c0 + tn]

    for jt in range(f // tn):
        a = conv(jt * tn)
        g = conv(f + jt * tn)
        act_ref[:, jt * tn:(jt + 1) * tn] = (a * jax.nn.sigmoid(a) * g).astype(act_ref.dtype)
    out = jnp.dot(act_ref[...], wd_ref[...], preferred_element_type=F32)
    o_ref[0] = x_ref[0] + g2_ref[0] * (_rms(out) * gpost_ref[...])


def _ffn(x, shift, scale, g2, g_pre, g_post, w_up, conv_w, conv_b, w_down, li):
    b, l, d = x.shape
    f = w_down.shape[1]
    tm = min(512, l)
    tn = 256
    nrow = l // tm
    hb = tm // SUBLANES
    row = lambda bi, i: (bi, i, 0)
    vec = lambda bi, i: (bi, 0, 0)
    c2 = lambda bi, i: (0, 0)
    resident = lambda shape: pl.BlockSpec((None,) + shape, lambda bi, i: (li, 0, 0),
                                          pipeline_mode=pl.Buffered(1))
    return pl.pallas_call(
        functools.partial(_ffn_kernel, nrow=nrow, tm=tm, tn=tn),
        out_shape=jax.ShapeDtypeStruct((b, l, d), F32),
        grid=(b, nrow),
        in_specs=[pl.BlockSpec((1, tm, d), row),
                  pl.BlockSpec((1, SUBLANES, d), lambda bi, i: (bi, jnp.maximum(i * hb - 1, 0), 0)),
                  pl.BlockSpec((1, SUBLANES, d), lambda bi, i: (bi, jnp.minimum((i + 1) * hb, l // SUBLANES - 1), 0)),
                  pl.BlockSpec((1, 1, d), vec),
                  pl.BlockSpec((1, 1, d), vec),
                  pl.BlockSpec((1, 1, d), vec),
                  pl.BlockSpec((1, d), c2),
                  pl.BlockSpec((1, d), c2),
                  resident((d, 2 * f)),
                  resident((CONV_W, 2 * f)),
                  resident((1, 2 * f)),
                  resident((f, d))],
        out_specs=pl.BlockSpec((1, tm, d), row),
        scratch_shapes=[pltpu.VMEM((tm + 2 * SUBLANES, d), MXU_DT),
                        pltpu.VMEM((tm, f), MXU_DT)],
        compiler_params=_cparams("parallel", "parallel"),
        name="conv_ffn",
    )(x, x, x, shift, scale, g2, g_pre, g_post, w_up, conv_w, conv_b, w_down)


def _rope_tables(l):
    rows = l // GRID_W
    row = jnp.repeat(jnp.arange(rows, dtype=F32), GRID_W)
    col = jnp.tile(jnp.arange(GRID_W, dtype=F32), rows)
    nf = HEAD_DIM // 4
    inv = ROPE_BASE ** (-jnp.arange(nf, dtype=F32) / nf)
    ang = jnp.concatenate([row[:, None] * inv, col[:, None] * inv], axis=-1)
    cos, sin = jnp.cos(ang), jnp.sin(ang)
    cos = jnp.tile(jnp.concatenate([cos, cos], axis=-1), (1, LANES // HEAD_DIM))
    sin = jnp.tile(jnp.concatenate([-sin, sin], axis=-1), (1, LANES // HEAD_DIM))
    return cos, sin


def _lower_bounds(logits):
    pr = jax.nn.softmax(logits.astype(F32), axis=0)
    cs = jnp.cumsum(pr, axis=0)
    return cs - cs[:1]


def kernel(x, c, ctx, c_ctx, w_mod, b_mod, norm_g, w_in, s5_lam_re, s5_lam_im, s5_log_dt, s5_b_re, s5_b_im, s5_c_re, s5_c_im, s5_d, s5_w_glu, s5_b_glu, att_sink, hg_lb_logits, hg_norm_g, w_branch, w_out, ffn_w_up, ffn_conv_w, ffn_conv_b, ffn_w_down):
    b, l, d = x.shape
    lc = ctx.shape[1]
    depth = w_in.shape[0]
    nc = c.shape[0]
    nrows = -(-(nc + 1) // SUBLANES) * SUBLANES
    cond = jnp.zeros((nrows, d), F32).at[:nc].set(c).at[nc].set(c_ctx)
    mods = _mod_all(cond, w_mod, b_mod)
    cos_l, sin_l = _rope_tables(l)
    cos_c = jnp.ones((lc, LANES), F32)
    sin_c = jnp.zeros((lc, LANES), F32)
    lb_all = _lower_bounds(hg_lb_logits)
    tables_all = jax.vmap(_s5_tables)(s5_lam_re, s5_lam_im, s5_log_dt, s5_b_re, s5_b_im,
                                      s5_c_re, s5_c_im, s5_d)

    w_in_b = w_in.astype(MXU_DT)
    w_glu = s5_w_glu.astype(MXU_DT)
    wbr = w_branch.astype(MXU_DT)
    wout = w_out.astype(MXU_DT)
    wup = ffn_w_up.astype(MXU_DT)
    wdn = ffn_w_down.astype(MXU_DT)
    cw = ffn_conv_w.astype(F32)
    cb = ffn_conv_b.astype(F32).reshape(depth, 1, -1)

    xl, xc = x, ctx
    for li in range(depth):
        with_ctx_out = li < depth - 1
        ml = mods[li, :nc].reshape(nc, 1, 6 * d)
        mc = jnp.broadcast_to(mods[li, nc].reshape(1, 1, 6 * d), (b, 1, 6 * d))
        sl = lambda m, k: m[:, :, k * d:(k + 1) * d]
        gains = norm_g[li].astype(F32)
        sink = att_sink[li].astype(F32) * LOG2E
        b_glu = s5_b_glu[li].astype(F32).reshape(1, BRANCH_W)
        hgn = hg_norm_g[li].astype(F32).reshape(1, BRANCH_W)
        lb = lb_all[li]

        pc = _win(xc, sl(mc, 0), sl(mc, 1), gains[0:1], w_in_b, li, cos_c, sin_c)
        u_c, q_c, k_c, v_c, hq_c, ff_c, fb_c, hi_c, hg_c, gate_c = pc
        zero_h = jnp.zeros((b, S5_NQ, 1, 4 * S5_SW), F32)
        y5_c, h_ctx = _s5(u_c, tables_all, li, zero_h)
        zero_s = jnp.zeros((b, 2, HG_HEADS, HG_D, HG_D), F32)
        of_c, ob_c, s_ctx = _hgrn(hq_c, hi_c, ff_c, fb_c, lb, zero_s)

        pl_ = _win(xl, sl(ml, 0), sl(ml, 1), gains[0:1], w_in_b, li, cos_l, sin_l)
        u_l, q_l, k_l, v_l, hq_l, ff_l, fb_l, hi_l, hg_l, gate_l = pl_
        y5_l, _ = _s5(u_l, tables_all, li, h_ctx)
        att_l = _attn(q_l, k_l, v_l, k_c, v_c, sink)
        of_l, ob_l, _ = _hgrn(hq_l, hi_l, ff_l, fb_l, lb, s_ctx)
        xl = _merge(xl, sl(ml, 2), y5_l, att_l, of_l, ob_l, hg_l, gate_l,
                    w_glu, b_glu, hgn, wbr, wout, li, gains[1:2])
        xl = _ffn(xl, sl(ml, 3), sl(ml, 4), sl(ml, 5), gains[2:3], gains[3:4], wup, cw, cb, wdn, li)

        if with_ctx_out:
            att_c = _attn_ctx(q_c, k_c, v_c, sink)
            xc = _merge(xc, sl(mc, 2), y5_c, att_c, of_c, ob_c, hg_c, gate_c,
                        w_glu, b_glu, hgn, wbr, wout, li, gains[1:2])
            xc = _ffn(xc, sl(mc, 3), sl(mc, 4), sl(mc, 5), gains[2:3], gains[3:4], wup, cw, cb, wdn, li)
    return xl
```
